```python
import math
import jax, jax.numpy as jnp
from jax import lax
import numpy as np

D_MODEL = 4096
BATCH = 4
SEQ = 2048
DEPTH = 2
DEC_BATCH = 128
DEC_SEQ = 4
PAST_LEN = 16384
PAGE_SIZE = 128

N_MIXERS = 2
N_GDN = (DEPTH + 1) // 2
N_RWKV = DEPTH // 2
GDN_HK = 32
GDN_HV = 64
GDN_DK = 128
GDN_DV = 128
GDN_KDIM = GDN_HK * GDN_DK
GDN_VDIM = GDN_HV * GDN_DV
GDN_CONV_DIM = 2 * GDN_KDIM + GDN_VDIM
GDN_IN_DIM = GDN_CONV_DIM + GDN_VDIM + 2 * GDN_HV
GDN_CONV_W = 4
GDN_CHUNK = 64
RWKV_N = 64
RWKV_H = D_MODEL // RWKV_N
RWKV_DECAY_LORA = max(32, int(round(D_MODEL ** 0.5 * 1.8 / 32)) * 32)
RWKV_AAA_LORA = max(32, int(round(D_MODEL ** 0.5 * 1.8 / 32)) * 32)
RWKV_GATE_LORA = max(32, int(round(D_MODEL ** 0.6 * 0.6 / 32)) * 32)
RWKV_LNX_EPS = 64e-5
D_FF = -(-(8 * D_MODEL) // (3 * 256)) * 256
RMS_EPS = 1e-6
L2_EPS = 1e-6

kernel_name = 'hybrid_gdn_rwkv7_decode_step'

F32 = jnp.float32


def rmsnorm(x, w):
    xf = x.astype(F32)
    y = xf * lax.rsqrt(jnp.mean(xf * xf, axis=-1, keepdims=True) + RMS_EPS)
    return (y * w.astype(F32)).astype(x.dtype)


def l2norm(x):
    return x * lax.rsqrt(jnp.sum(x * x, axis=-1, keepdims=True) + L2_EPS)


def swiglu(h, w_gate, w_up, w_down):
    return (jax.nn.silu(h @ w_gate) * (h @ w_up)) @ w_down


def gated_delta_rule(q, k, v, g, beta, s0):
    b, h, l, dk = q.shape
    dv = v.shape[-1]
    c = GDN_CHUNK
    n = -(-l // c)
    pad = n * c - l
    if pad:
        pw = ((0, 0), (0, 0), (0, pad), (0, 0))
        q, k, v = jnp.pad(q, pw), jnp.pad(k, pw), jnp.pad(v, pw)
        g = jnp.pad(g, pw[:3])
        beta = jnp.pad(beta, pw[:3])
    q = q * (dk ** -0.5)
    kb = k * beta[..., None]
    vb = (v * beta[..., None]).reshape(b, h, n, c, dv)
    q, k, kb = (t.reshape(b, h, n, c, dk) for t in (q, k, kb))
    g = jnp.cumsum(g.reshape(b, h, n, c), axis=-1)
    incl = jnp.tril(jnp.ones((c, c), bool))
    strict = jnp.tril(jnp.ones((c, c), bool), -1)
    decay = jnp.exp(jnp.where(incl, g[..., :, None] - g[..., None, :], -jnp.inf))
    m = jnp.where(strict, jnp.einsum('bhncd,bhnsd->bhncs', kb, k) * decay, 0.0)
    eye = jnp.eye(c, dtype=m.dtype)
    tinv = lax.linalg.triangular_solve(eye + m, jnp.broadcast_to(eye, m.shape),
                                       left_side=True, lower=True, unit_diagonal=True)
    u = jnp.einsum('bhncs,bhnsv->bhncv', tinv, vb)
    w = jnp.einsum('bhncs,bhnsd->bhncd', tinv, kb * jnp.exp(g)[..., None])
    a_intra = jnp.einsum('bhncd,bhnsd->bhncs', q, k) * decay
    qg = q * jnp.exp(g)[..., None]
    kg = k * jnp.exp(g[..., -1:] - g)[..., None]
    gl = jnp.exp(g[..., -1])

    def step(s, inp):
        w_i, u_i, qg_i, kg_i, a_i, gl_i = inp
        v_new = u_i - jnp.einsum('bhcd,bhdv->bhcv', w_i, s)
        o = jnp.einsum('bhcd,bhdv->bhcv', qg_i, s) + jnp.einsum('bhcs,bhsv->bhcv', a_i, v_new)
        s = s * gl_i[..., None, None] + jnp.einsum('bhcd,bhcv->bhdv', kg_i, v_new)
        return s, o

    xs = tuple(jnp.moveaxis(t, 2, 0) for t in (w, u, qg, kg, a_intra, gl))
    s, o = lax.scan(step, s0, xs)
    o = jnp.moveaxis(o, 0, 2).reshape(b, h, n * c, dv)[:, :, :l]
    return o, s


def gdn_mixer(h, conv_buf, s0, w_in, conv_w, a_log, dt_bias, norm_w, w_out):
    b, l, _ = h.shape
    proj = h @ w_in
    qkv = proj[..., :GDN_CONV_DIM]
    z = proj[..., GDN_CONV_DIM:GDN_CONV_DIM + GDN_VDIM]
    b_in = proj[..., GDN_CONV_DIM + GDN_VDIM:GDN_CONV_DIM + GDN_VDIM + GDN_HV]
    a_in = proj[..., GDN_CONV_DIM + GDN_VDIM + GDN_HV:]
    full = jnp.concatenate([conv_buf.astype(qkv.dtype), qkv], axis=1)
    conv = sum(full[:, j:j + l] * conv_w[j] for j in range(GDN_CONV_W))
    new_buf = full[:, l:]
    qkv_c = jax.nn.silu(conv).astype(F32)
    q = l2norm(qkv_c[..., :GDN_KDIM].reshape(b, l, GDN_HK, GDN_DK))
    k = l2norm(qkv_c[..., GDN_KDIM:2 * GDN_KDIM].reshape(b, l, GDN_HK, GDN_DK))
    v = qkv_c[..., 2 * GDN_KDIM:].reshape(b, l, GDN_HV, GDN_DV)
    rep = GDN_HV // GDN_HK
    q = jnp.repeat(q, rep, axis=2)
    k = jnp.repeat(k, rep, axis=2)
    beta = jax.nn.sigmoid(b_in.astype(F32))
    g = -jnp.exp(a_log.astype(F32)) * jax.nn.softplus(a_in.astype(F32) + dt_bias.astype(F32))
    o, s = gated_delta_rule(jnp.swapaxes(q, 1, 2), jnp.swapaxes(k, 1, 2), jnp.swapaxes(v, 1, 2),
                            jnp.swapaxes(g, 1, 2), jnp.swapaxes(beta, 1, 2), s0.astype(F32))
    o = jnp.swapaxes(o, 1, 2)
    o = o * lax.rsqrt(jnp.mean(o * o, axis=-1, keepdims=True) + RMS_EPS) * norm_w.astype(F32)
    o = o * jax.nn.silu(z.astype(F32).reshape(b, l, GDN_HV, GDN_DV))
    out = o.reshape(b, l, GDN_VDIM).astype(h.dtype) @ w_out
    return out, new_buf, s


def rwkv7_recurrence(r, w, k, v, a, bb, s0):
    def step(s, inp):
        r_t, w_t, k_t, v_t, a_t, b_t = inp
        sa = jnp.einsum('bhvk,bhk->bhv', s, a_t)
        s = s * w_t[:, :, None, :] + sa[..., None] * b_t[:, :, None, :] + v_t[..., None] * k_t[:, :, None, :]
        return s, jnp.einsum('bhvk,bhk->bhv', s, r_t)

    xs = tuple(jnp.moveaxis(t, 1, 0) for t in (r, w, k, v, a, bb))
    s, y = lax.scan(step, s0, xs)
    return jnp.moveaxis(y, 0, 1), s


def rwkv7_mixer(h, shift, s0, mix, w_r, w_k, w_v, w0, w1, w2, a0, a1, a2, g1, g2,
                k_k, k_a, r_k, lnx_w, lnx_b, w_out):
    b, l, d = h.shape
    prev = jnp.concatenate([shift[:, None, :].astype(h.dtype), h[:, :-1]], axis=1)
    xx = prev - h
    xr, xw, xk, xv, xa, xg = (h + xx * mix[i] for i in range(6))
    r = xr @ w_r
    k = xk @ w_k
    v = xv @ w_v
    w_log = -jax.nn.softplus(-(w0 + jnp.tanh(xw @ w1) @ w2).astype(F32)) - 0.5
    decay = jnp.exp(-jnp.exp(w_log))
    a = jax.nn.sigmoid((a0 + (xa @ a1) @ a2).astype(F32))
    g = jax.nn.sigmoid(xg @ g1) @ g2
    hd = (b, l, RWKV_H, RWKV_N)
    kk = l2norm((k * k_k).astype(F32).reshape(hd))
    k = (k.astype(F32) * (1.0 + (a - 1.0) * k_a.astype(F32))).reshape(hd)
    r = r.astype(F32).reshape(hd)
    v = v.astype(F32).reshape(hd)
    a = a.reshape(hd)
    y, s = rwkv7_recurrence(r, decay.reshape(hd), k, v, -kk, kk * a, s0.astype(F32))
    mu = jnp.mean(y, axis=-1, keepdims=True)
    var = jnp.mean(jnp.square(y - mu), axis=-1, keepdims=True)
    y = (y - mu) * lax.rsqrt(var + RWKV_LNX_EPS) * lnx_w.astype(F32).reshape(RWKV_H, RWKV_N) \
        + lnx_b.astype(F32).reshape(RWKV_H, RWKV_N)
    y = y + jnp.sum(r * k * r_k.astype(F32), axis=-1, keepdims=True) * v
    out = (y.reshape(b, l, d).astype(h.dtype) * g) @ w_out
    return out, h[:, -1], s


def trunk(x, conv_st, ssm_st, shift_st, wkv_st, p):
    convs, ssms, shifts, wkvs = [], [], [], []
    for i in range(DEPTH):
        j = i // N_MIXERS
        h = rmsnorm(x, p['norm_mix'][i])
        if i % N_MIXERS == 0:
            out, c_new, s_new = gdn_mixer(h, conv_st[j], ssm_st[j], p['gdn_w_in'][j], p['gdn_conv_w'][j],
                                          p['gdn_a_log'][j], p['gdn_dt_bias'][j], p['gdn_norm_w'][j],
                                          p['gdn_w_out'][j])
            convs.append(c_new.astype(conv_st.dtype))
            ssms.append(s_new.astype(ssm_st.dtype))
        else:
            out, sh_new, s_new = rwkv7_mixer(h, shift_st[j], wkv_st[j], p['rwkv_mix'][j], p['rwkv_w_r'][j],
                                             p['rwkv_w_k'][j], p['rwkv_w_v'][j], p['rwkv_w0'][j],
                                             p['rwkv_w1'][j], p['rwkv_w2'][j], p['rwkv_a0'][j],
                                             p['rwkv_a1'][j], p['rwkv_a2'][j], p['rwkv_g1'][j],
                                             p['rwkv_g2'][j], p['rwkv_k_k'][j], p['rwkv_k_a'][j],
                                             p['rwkv_r_k'][j], p['rwkv_lnx_w'][j], p['rwkv_lnx_b'][j],
                                             p['rwkv_w_out'][j])
            shifts.append(sh_new.astype(shift_st.dtype))
            wkvs.append(s_new.astype(wkv_st.dtype))
        x = x + out
        x = x + swiglu(rmsnorm(x, p['norm_ffn'][i]), p['ffn_w_gate'][i], p['ffn_w_up'][i], p['ffn_w_down'][i])
    y = rmsnorm(x, p['norm_final'])
    return y, jnp.stack(convs), jnp.stack(ssms), jnp.stack(shifts), jnp.stack(wkvs)


def setup_inputs(seed: int = 0) -> dict:
    key = jax.random.key(seed)
    ks = iter(jax.random.split(key, 64))

    def nrm(shape, scale):
        return jax.random.normal(next(ks), shape, F32) * scale

    def unif(shape, lo, hi):
        return jax.random.uniform(next(ks), shape, F32, lo, hi)

    D = D_MODEL
    dt = jnp.exp(unif((N_GDN, GDN_HV), math.log(1e-3), math.log(1e-1)))
    return {
        'x_prompt': nrm((BATCH, SEQ, D), 1.0),
        'x_sample': nrm((DEC_BATCH, DEC_SEQ, D), 1.0),
        'state_gdn_conv': nrm((N_GDN, DEC_BATCH, GDN_CONV_W - 1, GDN_CONV_DIM), 1.0),
        'state_gdn_ssm': nrm((N_GDN, DEC_BATCH, GDN_HV, GDN_DK, GDN_DV), 0.1),
        'state_rwkv_shift': nrm((N_RWKV, DEC_BATCH, D), 1.0),
        'state_rwkv_wkv': nrm((N_RWKV, DEC_BATCH, RWKV_H, RWKV_N, RWKV_N), 0.1),
        'norm_mix': 1.0 + nrm((DEPTH, D), 0.02),
        'norm_ffn': 1.0 + nrm((DEPTH, D), 0.02),
        'norm_final': 1.0 + nrm((D,), 0.02),
        'gdn_w_in': nrm((N_GDN, D, GDN_IN_DIM), D ** -0.5),
        'gdn_conv_w': nrm((N_GDN, GDN_CONV_W, GDN_CONV_DIM), GDN_CONV_W ** -0.5),
        'gdn_a_log': jnp.log(unif((N_GDN, GDN_HV), 1.0, 16.0)),
        'gdn_dt_bias': dt + jnp.log(-jnp.expm1(-dt)),
        'gdn_norm_w': 1.0 + nrm((N_GDN, GDN_DV), 0.02),
        'gdn_w_out': nrm((N_GDN, GDN_VDIM, D), GDN_VDIM ** -0.5),
        'rwkv_mix': unif((N_RWKV, 6, D), 0.0, 1.0),
        'rwkv_w_r': nrm((N_RWKV, D, D), D ** -0.5),
        'rwkv_w_k': nrm((N_RWKV, D, D), D ** -0.5),
        'rwkv_w_v': nrm((N_RWKV, D, D), D ** -0.5),
        'rwkv_w0': unif((N_RWKV, D), -6.0, -1.0) + 0.5,
        'rwkv_w1': nrm((N_RWKV, D, RWKV_DECAY_LORA), D ** -0.5),
        'rwkv_w2': nrm((N_RWKV, RWKV_DECAY_LORA, D), 0.1 * RWKV_DECAY_LORA ** -0.5),
        'rwkv_a0': nrm((N_RWKV, D), 0.1),
        'rwkv_a1': nrm((N_RWKV, D, RWKV_AAA_LORA), D ** -0.5),
        'rwkv_a2': nrm((N_RWKV, RWKV_AAA_LORA, D), 0.1 * RWKV_AAA_LORA ** -0.5),
        'rwkv_g1': nrm((N_RWKV, D, RWKV_GATE_LORA), D ** -0.5),
        'rwkv_g2': nrm((N_RWKV, RWKV_GATE_LORA, D), RWKV_GATE_LORA ** -0.5),
        'rwkv_k_k': 0.85 + nrm((N_RWKV, D), 0.02),
        'rwkv_k_a': 1.0 + nrm((N_RWKV, D), 0.02),
        'rwkv_r_k': nrm((N_RWKV, RWKV_H, RWKV_N), 0.1),
        'rwkv_lnx_w': 1.0 + nrm((N_RWKV, D), 0.02),
        'rwkv_lnx_b': nrm((N_RWKV, D), 0.02),
        'rwkv_w_out': nrm((N_RWKV, D, D), D ** -0.5),
        'ffn_w_gate': nrm((DEPTH, D, D_FF), D ** -0.5),
        'ffn_w_up': nrm((DEPTH, D, D_FF), D ** -0.5),
        'ffn_w_down': nrm((DEPTH, D_FF, D), D_FF ** -0.5),
    }


def reference(x_prompt, x_sample, state_gdn_conv, state_gdn_ssm, state_rwkv_shift, state_rwkv_wkv,
              norm_mix, norm_ffn, norm_final,
              gdn_w_in, gdn_conv_w, gdn_a_log, gdn_dt_bias, gdn_norm_w, gdn_w_out,
              rwkv_mix, rwkv_w_r, rwkv_w_k, rwkv_w_v, rwkv_w0, rwkv_w1, rwkv_w2,
              rwkv_a0, rwkv_a1, rwkv_a2, rwkv_g1, rwkv_g2, rwkv_k_k, rwkv_k_a, rwkv_r_k,
              rwkv_lnx_w, rwkv_lnx_b, rwkv_w_out,
              ffn_w_gate, ffn_w_up, ffn_w_down):
    p = dict(norm_mix=norm_mix, norm_ffn=norm_ffn, norm_final=norm_final,
             gdn_w_in=gdn_w_in, gdn_conv_w=gdn_conv_w, gdn_a_log=gdn_a_log, gdn_dt_bias=gdn_dt_bias,
             gdn_norm_w=gdn_norm_w, gdn_w_out=gdn_w_out,
             rwkv_mix=rwkv_mix, rwkv_w_r=rwkv_w_r, rwkv_w_k=rwkv_w_k, rwkv_w_v=rwkv_w_v,
             rwkv_w0=rwkv_w0, rwkv_w1=rwkv_w1, rwkv_w2=rwkv_w2, rwkv_a0=rwkv_a0, rwkv_a1=rwkv_a1,
             rwkv_a2=rwkv_a2, rwkv_g1=rwkv_g1, rwkv_g2=rwkv_g2, rwkv_k_k=rwkv_k_k, rwkv_k_a=rwkv_k_a,
             rwkv_r_k=rwkv_r_k, rwkv_lnx_w=rwkv_lnx_w, rwkv_lnx_b=rwkv_lnx_b, rwkv_w_out=rwkv_w_out,
             ffn_w_gate=ffn_w_gate, ffn_w_up=ffn_w_up, ffn_w_down=ffn_w_down)
    bp = x_prompt.shape[0]
    conv0 = jnp.zeros((N_GDN, bp) + state_gdn_conv.shape[2:], state_gdn_conv.dtype)
    ssm0 = jnp.zeros((N_GDN, bp) + state_gdn_ssm.shape[2:], state_gdn_ssm.dtype)
    shift0 = jnp.zeros((N_RWKV, bp) + state_rwkv_shift.shape[2:], state_rwkv_shift.dtype)
    wkv0 = jnp.zeros((N_RWKV, bp) + state_rwkv_wkv.shape[2:], state_rwkv_wkv.dtype)
    y_prompt, conv_p, ssm_p, shift_p, wkv_p = trunk(x_prompt, conv0, ssm0, shift0, wkv0, p)
    y_sample, conv_s, ssm_s, shift_s, wkv_s = trunk(x_sample, state_gdn_conv, state_gdn_ssm,
                                                     state_rwkv_shift, state_rwkv_wkv, p)
    return (y_prompt, y_sample, conv_p, ssm_p, shift_p, wkv_p, conv_s, ssm_s, shift_s, wkv_s)
```

```python
import functools

import jax
import jax.numpy as jnp
from jax import lax
from jax.experimental import pallas as pl
from jax.experimental.pallas import tpu as pltpu

F32 = jnp.float32
BF16 = jnp.bfloat16
HIGHEST = lax.Precision.HIGHEST

D_MODEL = 4096
DEPTH = 2
GDN_HK = 32
GDN_HV = 64
GDN_DK = 128
GDN_DV = 128
GDN_CONV_W = 4
GDN_CHUNK = 64
RWKV_N = 64
RWKV_CHUNK = 64
RWKV_LNX_EPS = 64e-5
RMS_EPS = 1e-6
L2_EPS = 1e-6

LANES = 128
SUBLANES = 8
VMEM_LIMIT = 56 * 1024 * 1024


def _cparams(sem):
    return pltpu.CompilerParams(dimension_semantics=sem, vmem_limit_bytes=VMEM_LIMIT)


def _pick(n, prefs):
    for p in prefs:
        if n % p == 0:
            return p
    return n


def _dot(a, b, precision=None):
    return jnp.dot(a, b, preferred_element_type=F32, precision=precision)


def _dot_nt(a, b, precision=None):
    return lax.dot_general(a, b, (((1,), (1,)), ((), ())), preferred_element_type=F32, precision=precision)


def _dot_tn(a, b, precision=None):
    return lax.dot_general(a, b, (((0,), (0,)), ((), ())), preferred_element_type=F32, precision=precision)


def _sigmoid(x):
    return 1.0 / (1.0 + jnp.exp(-x))


def _softplus(x):
    return jnp.maximum(x, 0.0) + jnp.log1p(jnp.exp(-jnp.abs(x)))


def _tri_masks(c):
    ii = lax.broadcasted_iota(jnp.int32, (c, c), 0)
    jj = lax.broadcasted_iota(jnp.int32, (c, c), 1)
    return ii >= jj, ii > jj, ii == jj


def _unit_lower_inverse(m, c):
    _, _, diag = _tri_masks(c)
    x = jnp.where(diag, 1.0, 0.0).astype(F32) - m
    p = m
    power = 1
    while 2 * power < c:
        p = _dot(p, p, HIGHEST)
        power *= 2
        x = x + _dot(x, p, HIGHEST)
    return x


def _mm_kernel(*refs, nk, has_res):
    if has_res:
        a_ref, b_ref, r_ref, o_ref = refs[:4]
        scratch = refs[4:]
    else:
        a_ref, b_ref, o_ref = refs[:3]
        r_ref = None
        scratch = refs[3:]
    part = _dot(a_ref[...], b_ref[...])

    def finish(total):
        if has_res:
            total = total + r_ref[...]
        o_ref[...] = total.astype(o_ref.dtype)

    if nk == 1:
        finish(part)
        return
    acc_ref = scratch[0]
    k = pl.program_id(2)

    @pl.when(k == 0)
    def _():
        acc_ref[...] = part

    @pl.when(jnp.logical_and(k > 0, k < nk - 1))
    def _():
        acc_ref[...] += part

    @pl.when(k == nk - 1)
    def _():
        finish(acc_ref[...] + part)


def matmul(a, b, res=None, out_dtype=F32, bm=None, bn=None, bk=None):
    m, kdim = a.shape
    _, n = b.shape
    bm = bm or _pick(m, (1024, 512, 256, 128, 64, 32, 16, 8))
    bk = bk or kdim
    bn = bn or _pick(n, (1024, 512, 256, 128) if (res is None and bk == kdim) else (512, 256, 128))
    nk = kdim // bk
    in_specs = [pl.BlockSpec((bm, bk), lambda i, j, k: (i, k)),
                pl.BlockSpec((bk, bn), lambda i, j, k: (k, j))]
    args = [a, b]
    if res is not None:
        in_specs.append(pl.BlockSpec((bm, bn), lambda i, j, k: (i, j)))
        args.append(res)
    scratch = [pltpu.VMEM((bm, bn), F32)] if nk > 1 else []
    return pl.pallas_call(
        functools.partial(_mm_kernel, nk=nk, has_res=res is not None),
        grid=(m // bm, n // bn, nk),
        in_specs=in_specs,
        out_specs=pl.BlockSpec((bm, bn), lambda i, j, k: (i, j)),
        out_shape=jax.ShapeDtypeStruct((m, n), out_dtype),
        scratch_shapes=scratch,
        compiler_params=_cparams(("parallel", "parallel", "arbitrary")),
        name="matmul",
    )(*args)


def _gate_up_kernel(h_ref, wg_ref, wu_ref, o_ref):
    h = h_ref[...]
    g = _dot(h, wg_ref[...])
    u = _dot(h, wu_ref[...])
    o_ref[...] = (g * _sigmoid(g) * u).astype(o_ref.dtype)


def gate_up(h, wg, wu):
    m, kdim = h.shape
    n = wg.shape[1]
    bm = _pick(m, (1024, 512, 256, 128, 64, 32, 16, 8))
    bn = _pick(n, (512, 256, 128))
    return pl.pallas_call(
        _gate_up_kernel,
        grid=(m // bm, n // bn),
        in_specs=[pl.BlockSpec((bm, kdim), lambda i, j: (i, 0)),
                  pl.BlockSpec((kdim, bn), lambda i, j: (0, j)),
                  pl.BlockSpec((kdim, bn), lambda i, j: (0, j))],
        out_specs=pl.BlockSpec((bm, bn), lambda i, j: (i, j)),
        out_shape=jax.ShapeDtypeStruct((m, n), BF16),
        compiler_params=_cparams(("parallel", "parallel")),
        name="ffn_gate_up",
    )(h, wg, wu)


def _rms_kernel(x_ref, w_ref, o_ref):
    x = x_ref[...]
    y = x * lax.rsqrt(jnp.mean(x * x, axis=-1, keepdims=True) + RMS_EPS)
    o_ref[...] = (y * w_ref[...]).astype(o_ref.dtype)


def rmsnorm(x, w, out_dtype):
    m, d = x.shape
    br = _pick(m, (256, 128, 64, 32, 16, 8))
    return pl.pallas_call(
        _rms_kernel,
        grid=(m // br,),
        in_specs=[pl.BlockSpec((br, d), lambda i: (i, 0)),
                  pl.BlockSpec((1, d), lambda i: (0, 0))],
        out_specs=pl.BlockSpec((br, d), lambda i: (i, 0)),
        out_shape=jax.ShapeDtypeStruct((m, d), out_dtype),
        compiler_params=_cparams(("parallel",)),
        name="rmsnorm",
    )(x, w.reshape(1, d))


def _gdn_prep_kernel(x_ref, buf_ref, cw_ref, o_ref, win_ref, *, lb, n_q_blocks, n_qk_blocks, q_scale):
    cblk = pl.program_id(1)
    l = pl.program_id(2)
    halo = GDN_CONV_W - 1
    top = SUBLANES

    @pl.when(l == 0)
    def _():
        win_ref[top - halo:top, :] = buf_ref[0]

    win_ref[top:top + lb, :] = x_ref[0]
    acc = win_ref[top - halo:top - halo + lb, :] * cw_ref[0:1, :]
    for j in range(1, GDN_CONV_W):
        acc = acc + win_ref[top - halo + j:top - halo + j + lb, :] * cw_ref[j:j + 1, :]
    carry = win_ref[top + lb - halo:top + lb, :]
    win_ref[top - halo:top, :] = carry
    y = acc * _sigmoid(acc)

    @pl.when(cblk >= n_qk_blocks)
    def _():
        o_ref[0] = y

    @pl.when(cblk < n_qk_blocks)
    def _():
        scale = jnp.where(cblk < n_q_blocks, q_scale, 1.0).astype(F32)
        for s in range(y.shape[1] // GDN_DK):
            ys = y[:, s * GDN_DK:(s + 1) * GDN_DK]
            inv = lax.rsqrt(jnp.sum(ys * ys, axis=-1, keepdims=True) + L2_EPS) * scale
            o_ref[0, :, s * GDN_DK:(s + 1) * GDN_DK] = ys * inv


def gdn_prep(proj, conv_buf, conv_w):
    b, l, _ = proj.shape
    kdim = GDN_HK * GDN_DK
    conv_dim = conv_w.shape[1]
    cb = _pick(kdim, (2048, 1024, 512, 256, 128))
    lb = _pick(l, (256, 128, 64, 32, 16, 8))
    kern = functools.partial(_gdn_prep_kernel, lb=lb, n_q_blocks=kdim // cb, n_qk_blocks=2 * kdim // cb,
                             q_scale=float(GDN_DK) ** -0.5)
    return pl.pallas_call(
        kern,
        grid=(b, conv_dim // cb, l // lb),
        in_specs=[pl.BlockSpec((1, lb, cb), lambda i, c, t: (i, t, c)),
                  pl.BlockSpec((1, GDN_CONV_W - 1, cb), lambda i, c, t: (i, 0, c)),
                  pl.BlockSpec((GDN_CONV_W, cb), lambda i, c, t: (0, c))],
        out_specs=pl.BlockSpec((1, lb, cb), lambda i, c, t: (i, t, c)),
        out_shape=jax.ShapeDtypeStruct((b, l, conv_dim), F32),
        scratch_shapes=[pltpu.VMEM((SUBLANES + lb, cb), F32)],
        compiler_params=_cparams(("parallel", "parallel", "arbitrary")),
        name="gdn_prep",
    )(proj, conv_buf, conv_w)


def _gdn_gates_kernel(ba_ref, alog_ref, dtb_ref, beta_ref, gc_ref, pad_ref, *, lb, lpb, chunk):
    hv = GDN_HV
    ba = ba_ref[0]
    beta = _sigmoid(ba[:, :hv])
    g = -jnp.exp(alog_ref[...]) * _softplus(ba[:, hv:2 * hv] + dtb_ref[...])
    if lpb != lb:
        pad_ref[...] = jnp.zeros_like(pad_ref)
        pad_ref[0:lb, :] = g
        g = pad_ref[...]
        pad_ref[0:lb, :] = beta
        beta = pad_ref[...]
    ii = lax.broadcasted_iota(jnp.int32, (lpb, lpb), 0)
    jj = lax.broadcasted_iota(jnp.int32, (lpb, lpb), 1)
    same_chunk_causal = jnp.logical_and(ii >= jj, ii // chunk == jj // chunk)
    gc = _dot(jnp.where(same_chunk_causal, 1.0, 0.0).astype(F32), g, HIGHEST)
    beta_ref[0] = beta
    gc_ref[0] = gc


def gdn_gates(ba, a_log, dt_bias, chunk, lp):
    b, l, w = ba.shape
    lb = _pick(l, (256, 128, 64, 32, 16, 8))
    lpb = lb if lp == l else lp
    kern = functools.partial(_gdn_gates_kernel, lb=lb, lpb=lpb, chunk=chunk)
    out = jax.ShapeDtypeStruct((b, lp, GDN_HV), F32)
    return pl.pallas_call(
        kern,
        grid=(b, l // lb),
        in_specs=[pl.BlockSpec((1, lb, w), lambda i, t: (i, t, 0)),
                  pl.BlockSpec((1, GDN_HV), lambda i, t: (0, 0)),
                  pl.BlockSpec((1, GDN_HV), lambda i, t: (0, 0))],
        out_specs=[pl.BlockSpec((1, lpb, GDN_HV), lambda i, t: (i, t, 0)),
                   pl.BlockSpec((1, lpb, GDN_HV), lambda i, t: (i, t, 0))],
        out_shape=[out, out],
        scratch_shapes=[pltpu.VMEM((lpb, GDN_HV), F32)],
        compiler_params=_cparams(("parallel", "parallel")),
        name="gdn_gates",
    )(ba, a_log.reshape(1, GDN_HV), dt_bias.reshape(1, GDN_HV))


def _delta_chunk(qc, kc, vc, beta_col, gc_col, gc_row, s, c):
    incl, strict, _ = _tri_masks(c)
    diff = gc_col - gc_row
    decay = jnp.where(incl, jnp.exp(jnp.where(incl, diff, 0.0)), 0.0)
    kb = kc * beta_col
    vb = vc * beta_col
    eg = jnp.exp(gc_col)
    a_intra = _dot_nt(qc, kc) * decay
    m = jnp.where(strict, _dot_nt(kb, kc) * decay, 0.0)
    tinv = _unit_lower_inverse(m, c)
    u = _dot(tinv, vb)
    w = _dot(tinv, kb * eg)
    qg = qc * eg
    v_new = u - _dot(w, s)
    o = _dot(qg, s) + _dot(a_intra, v_new)
    g_last = gc_col[c - 1:c, :]
    kg = kc * jnp.exp(g_last - gc_col)
    s_new = s * jnp.exp(g_last) + _dot_tn(kg, v_new)
    return o, s_new


def _gdn_delta_kernel(*refs, l_real, lb, chunk, hb, has_s0):
    if has_s0:
        q_ref, k_ref, v_ref, z_ref, beta_ref, gc_ref, gct_ref, nw_ref, s0_ref = refs[:9]
        rest = refs[9:]
    else:
        q_ref, k_ref, v_ref, z_ref, beta_ref, gc_ref, gct_ref, nw_ref = refs[:8]
        s0_ref = None
        rest = refs[8:]
    o_ref, so_ref, s_scr, pad_scr, o_scr = rest
    hblk = pl.program_id(1)
    t = pl.program_id(2)
    nt = pl.num_programs(2)
    padded = l_real < lb
    n_chunks = lb // chunk
    lane = lax.broadcasted_iota(jnp.int32, (chunk, GDN_HV), 1)

    def load_rows(ref, lanes, row0):
        if padded:
            pad_scr[...] = jnp.zeros_like(pad_scr)
            pad_scr[0:l_real, :] = ref[0, :, lanes]
            return pad_scr[...]
        return ref[0, pl.ds(row0, chunk), lanes]

    for j in range(hb):
        head = hblk * hb + j
        kslab = slice((j // 2) * GDN_DK, (j // 2 + 1) * GDN_DK) if hb > 1 else slice(0, GDN_DK)
        vslab = slice(j * GDN_DV, (j + 1) * GDN_DV)

        @pl.when(t == 0)
        def _():
            if has_s0:
                s_scr[j] = s0_ref[0, j]
            else:
                s_scr[j] = jnp.zeros((GDN_DK, GDN_DV), F32)

        def chunk_body(ci, carry):
            row0 = pl.multiple_of(ci * chunk, chunk)
            qc = load_rows(q_ref, kslab, row0)
            kc = load_rows(k_ref, kslab, row0)
            vc = load_rows(v_ref, vslab, row0)
            sel = lane == head
            beta_col = jnp.sum(jnp.where(sel, beta_ref[0, pl.ds(row0, chunk), :], 0.0), axis=-1, keepdims=True)
            gc_col = jnp.sum(jnp.where(sel, gc_ref[0, pl.ds(row0, chunk), :], 0.0), axis=-1, keepdims=True)
            gc_row = gct_ref[0, j, pl.ds(ci, 1), :]
            o, s_new = _delta_chunk(qc, kc, vc, beta_col, gc_col, gc_row, s_scr[j], chunk)
            s_scr[j] = s_new
            o = o * lax.rsqrt(jnp.mean(o * o, axis=-1, keepdims=True) + RMS_EPS) * nw_ref[...]
            if padded:
                o_scr[...] = o
                z = z_ref[0, :, vslab]
                o_ref[0, :, vslab] = (o_scr[0:l_real, :] * (z * _sigmoid(z))).astype(o_ref.dtype)
            else:
                z = z_ref[0, pl.ds(row0, chunk), vslab]
                o_ref[0, pl.ds(row0, chunk), vslab] = (o * (z * _sigmoid(z))).astype(o_ref.dtype)
            return carry

        if n_chunks == 1:
            chunk_body(0, 0)
        else:
            lax.fori_loop(0, n_chunks, chunk_body, 0)

        @pl.when(t == nt - 1)
        def _():
            so_ref[0, j] = s_scr[j]


def gdn_delta(qkv, proj, beta, gc, gct, norm_w, s0, l_real, chunk, lb, hb):
    b = qkv.shape[0]
    conv_dim = 2 * GDN_HK * GDN_DK + GDN_HV * GDN_DV
    lp = beta.shape[1]
    lrows = l_real if l_real < lb else lb
    nt = lp // lb
    kw = max(1, hb // 2)
    rep = GDN_HV // GDN_HK
    q_base = 0
    k_base = GDN_HK // kw
    v_base = (2 * GDN_HK * GDN_DK) // (hb * GDN_DV)
    z_base = conv_dim // (hb * GDN_DV)

    def kidx(h):
        return (h * hb) // rep // kw

    in_specs = [
        pl.BlockSpec((1, lrows, kw * GDN_DK), lambda i, h, t: (i, t, q_base + kidx(h))),
        pl.BlockSpec((1, lrows, kw * GDN_DK), lambda i, h, t: (i, t, k_base + kidx(h))),
        pl.BlockSpec((1, lrows, hb * GDN_DV), lambda i, h, t: (i, t, v_base + h)),
        pl.BlockSpec((1, lrows, hb * GDN_DV), lambda i, h, t: (i, t, z_base + h)),
        pl.BlockSpec((1, lb, GDN_HV), lambda i, h, t: (i, t, 0)),
        pl.BlockSpec((1, lb, GDN_HV), lambda i, h, t: (i, t, 0)),
        pl.BlockSpec((1, hb, lb // chunk, chunk), lambda i, h, t: (i, h, t, 0)),
        pl.BlockSpec((1, GDN_DV), lambda i, h, t: (0, 0)),
    ]
    args = [qkv, qkv, qkv, proj, beta, gc, gct, norm_w.reshape(1, GDN_DV)]
    if s0 is not None:
        in_specs.append(pl.BlockSpec((1, hb, GDN_DK, GDN_DV), lambda i, h, t: (i, h, 0, 0)))
        args.append(s0)
    kern = functools.partial(_gdn_delta_kernel, l_real=l_real, lb=lb, chunk=chunk, hb=hb, has_s0=s0 is not None)
    return pl.pallas_call(
        kern,
        grid=(b, GDN_HV // hb, nt),
        in_specs=in_specs,
        out_specs=[pl.BlockSpec((1, lrows, hb * GDN_DV), lambda i, h, t: (i, t, h)),
                   pl.BlockSpec((1, hb, GDN_DK, GDN_DV), lambda i, h, t: (i, h, 0, 0))],
        out_shape=[jax.ShapeDtypeStruct((b, l_real, GDN_HV * GDN_DV), BF16),
                   jax.ShapeDtypeStruct((b, GDN_HV, GDN_DK, GDN_DV), F32)],
        scratch_shapes=[pltpu.VMEM((hb, GDN_DK, GDN_DV), F32),
                        pltpu.VMEM((chunk, GDN_DK), F32),
                        pltpu.VMEM((chunk, GDN_DV), F32)],
        compiler_params=_cparams(("parallel", "parallel", "arbitrary")),
        name="gdn_delta",
    )(*args)


def _rwkv_mix_kernel(x_ref, nw_ref, shift_ref, mix_ref, o0, o1, o2, o3, o4, o5, last_ref, win_ref, *, lb):
    t = pl.program_id(1)
    top = SUBLANES

    @pl.when(t == 0)
    def _():
        win_ref[top - 1:top, :] = shift_ref[0]

    x = x_ref[0]
    h = x * lax.rsqrt(jnp.mean(x * x, axis=-1, keepdims=True) + RMS_EPS) * nw_ref[...]
    win_ref[top:top + lb, :] = h
    prev = win_ref[top - 1:top - 1 + lb, :]
    last = win_ref[top + lb - 1:top + lb, :]
    win_ref[top - 1:top, :] = last
    last_ref[0] = last
    xx = prev - h
    for i, o in enumerate((o0, o1, o2, o3, o4, o5)):
        o[0] = (h + xx * mix_ref[i:i + 1, :]).astype(o.dtype)


def rwkv_mix(x, norm_w, shift, mix):
    b, l, d = x.shape
    lb = _pick(l, (128, 64, 32, 16, 8))
    kern = functools.partial(_rwkv_mix_kernel, lb=lb)
    tok = pl.BlockSpec((1, lb, d), lambda i, t: (i, t, 0))
    row = pl.BlockSpec((1, 1, d), lambda i, t: (i, 0, 0))
    outs = pl.pallas_call(
        kern,
        grid=(b, l // lb),
        in_specs=[tok, pl.BlockSpec((1, d), lambda i, t: (0, 0)), row,
                  pl.BlockSpec((6, d), lambda i, t: (0, 0))],
        out_specs=[tok] * 6 + [row],
        out_shape=[jax.ShapeDtypeStruct((b, l, d), BF16)] * 6 + [jax.ShapeDtypeStruct((b, 1, d), F32)],
        scratch_shapes=[pltpu.VMEM((SUBLANES + lb, d), F32)],
        compiler_params=_cparams(("parallel", "arbitrary")),
        name="rwkv_mix",
    )(x, norm_w.reshape(1, d), shift.reshape(b, 1, d), mix)
    return outs[:6], outs[6].reshape(b, d)


def _lora_kernel(x_ref, w1_ref, w2_ref, b_ref, o_ref, *, act):
    hmid = _dot(x_ref[...], w1_ref[...])
    if act == "tanh":
        hmid = jnp.tanh(hmid)
    elif act == "sigmoid":
        hmid = _sigmoid(hmid)
    o_ref[...] = b_ref[...] + _dot(hmid.astype(BF16), w2_ref[...])


def lora(x, w1, w2, bias, act):
    m, d = x.shape
    r = w1.shape[1]
    n = w2.shape[1]
    bm = _pick(m, (256, 128, 64, 32, 16, 8))
    return pl.pallas_call(
        functools.partial(_lora_kernel, act=act),
        grid=(m // bm,),
        in_specs=[pl.BlockSpec((bm, d), lambda i: (i, 0)),
                  pl.BlockSpec((d, r), lambda i: (0, 0)),
                  pl.BlockSpec((r, n), lambda i: (0, 0)),
                  pl.BlockSpec((1, n), lambda i: (0, 0))],
        out_specs=pl.BlockSpec((bm, n), lambda i: (i, 0)),
        out_shape=jax.ShapeDtypeStruct((m, n), F32),
        compiler_params=_cparams(("parallel",)),
        name="rwkv_lora",
    )(x, w1, w2, bias.reshape(1, n))


def _wkv_chunk(r, k, v, lw, av, bv, s, c):
    incl, strict, _ = _tri_masks(c)
    cl = _dot(jnp.where(incl, 1.0, 0.0).astype(F32), lw, HIGHEST)
    ep = jnp.exp(cl)
    em = jnp.exp(-cl)
    a_h = av * jnp.exp(cl - lw)
    r_h = r * ep
    k_h = k * em
    b_h = bv * em
    a_ak = jnp.where(strict, _dot_nt(a_h, k_h), 0.0)
    a_ab = jnp.where(strict, _dot_nt(a_h, b_h), 0.0)
    a_rk = jnp.where(incl, _dot_nt(r_h, k_h), 0.0)
    a_rb = jnp.where(incl, _dot_nt(r_h, b_h), 0.0)
    tinv = _unit_lower_inverse(-a_ab, c)
    u = _dot(tinv, _dot_nt(a_h, s) + _dot(a_ak, v))
    y = _dot_nt(r_h, s) + _dot(a_rb, u) + _dot(a_rk, v)
    pc = ep[c - 1:c, :]
    s_new = s * pc + _dot_tn(u, b_h * pc) + _dot_tn(v, k_h * pc)
    return y, s_new


def _rwkv_wkv_kernel(*refs, l_real, lb, chunk, has_s0):
    names = 12 + (1 if has_s0 else 0)
    (r_ref, k_ref, v_ref, w_ref, a_ref, g_ref, kk_ref, ka_ref, rk_ref, lw_ref, lb_ref) = refs[:11]
    s0_ref = refs[11] if has_s0 else None
    o_ref, so_ref, s_scr, pad_scr = refs[names - 1:]
    t = pl.program_id(2)
    nt = pl.num_programs(2)
    n = RWKV_N
    padded = l_real < lb
    n_chunks = lb // chunk
    heads = LANES // n

    @pl.when(t == 0)
    def _():
        for j in range(heads):
            if has_s0:
                s_scr[j] = s0_ref[0, j]
            else:
                s_scr[j] = jnp.zeros((n, n), F32)

    def load_rows(ref, row0):
        if padded:
            pad_scr[...] = jnp.zeros_like(pad_scr)
            pad_scr[0:l_real, :] = ref[0]
            return pad_scr[...]
        return ref[0, pl.ds(row0, chunk), :]

    def chunk_body(ci, carry):
        row0 = pl.multiple_of(ci * chunk, chunk)
        r2 = load_rows(r_ref, row0)
        k2 = load_rows(k_ref, row0)
        v2 = load_rows(v_ref, row0)
        w2 = load_rows(w_ref, row0)
        a2 = load_rows(a_ref, row0)
        g2 = load_rows(g_ref, row0)
        lw2 = -jnp.exp(-_softplus(-w2) - 0.5)
        if padded:
            rows = lax.broadcasted_iota(jnp.int32, lw2.shape, 0)
            lw2 = jnp.where(rows < l_real, lw2, 0.0)
        a_sig = _sigmoid(a2)
        kkn = k2 * kk_ref[...]
        kmod = k2 * (1.0 + (a_sig - 1.0) * ka_ref[...])
        rkr = r2 * kmod * rk_ref[...]
        for j in range(heads):
            sl = slice(j * n, (j + 1) * n)
            kk = kkn[:, sl]
            kk = kk * lax.rsqrt(jnp.sum(kk * kk, axis=-1, keepdims=True) + L2_EPS)
            r = r2[:, sl]
            v = v2[:, sl]
            y, s_new = _wkv_chunk(r, kmod[:, sl], v, lw2[:, sl], -kk, kk * a_sig[:, sl], s_scr[j], chunk)
            s_scr[j] = s_new
            mu = jnp.mean(y, axis=-1, keepdims=True)
            yc = y - mu
            var = jnp.mean(yc * yc, axis=-1, keepdims=True)
            yn = yc * lax.rsqrt(var + RWKV_LNX_EPS) * lw_ref[:, sl] + lb_ref[:, sl]
            yn = yn + jnp.sum(rkr[:, sl], axis=-1, keepdims=True) * v
            out = (yn * g2[:, sl]).astype(o_ref.dtype)
            if padded:
                o_ref[0, :, sl] = out[0:l_real, :]
            else:
                o_ref[0, pl.ds(row0, chunk), sl] = out
        return carry

    if n_chunks == 1:
        chunk_body(0, 0)
    else:
        lax.fori_loop(0, n_chunks, chunk_body, 0)

    @pl.when(t == nt - 1)
    def _():
        for j in range(heads):
            so_ref[0, j] = s_scr[j]


def rwkv_wkv(r, k, v, w, a, g, k_k, k_a, r_k, lnx_w, lnx_b, s0, chunk, lb):
    b, l, d = r.shape
    n = RWKV_N
    h = d // n
    heads = LANES // n
    lrows = l if l < lb else lb
    nt = max(1, l // lb)
    tok = pl.BlockSpec((1, lrows, LANES), lambda i, p, t: (i, t, p))
    par = pl.BlockSpec((1, LANES), lambda i, p, t: (0, p))
    st = pl.BlockSpec((1, heads, n, n), lambda i, p, t: (i, p, 0, 0))
    in_specs = [tok] * 6 + [par] * 5
    args = [r, k, v, w, a, g, k_k.reshape(1, d), k_a.reshape(1, d), r_k.reshape(1, d),
            lnx_w.reshape(1, d), lnx_b.reshape(1, d)]
    if s0 is not None:
        in_specs.append(st)
        args.append(s0)
    kern = functools.partial(_rwkv_wkv_kernel, l_real=l, lb=lb, chunk=chunk, has_s0=s0 is not None)
    return pl.pallas_call(
        kern,
        grid=(b, h // heads, nt),
        in_specs=in_specs,
        out_specs=[tok, st],
        out_shape=[jax.ShapeDtypeStruct((b, l, d), BF16),
                   jax.ShapeDtypeStruct((b, h, n, n), F32)],
        scratch_shapes=[pltpu.VMEM((heads, n, n), F32), pltpu.VMEM((chunk, LANES), F32)],
        compiler_params=_cparams(("parallel", "parallel", "arbitrary")),
        name="rwkv_wkv",
    )(*args)


def _round_up(x, m):
    return -(-x // m) * m


def _gdn_layer(x, conv_st, ssm_st, w):
    b, l, d = x.shape
    conv_dim = 2 * GDN_HK * GDN_DK + GDN_HV * GDN_DV
    x2 = x.reshape(b * l, d)
    h = rmsnorm(x2, w["norm_mix"], BF16)
    proj = matmul(h, w["w_in_main"]).reshape(b, l, -1)
    ba = matmul(h, w["w_in_ba"]).reshape(b, l, -1)
    if conv_st is None:
        conv_st = jnp.zeros((b, GDN_CONV_W - 1, conv_dim), F32)
    new_conv = proj[:, l - (GDN_CONV_W - 1):, :conv_dim]
    qkv = gdn_prep(proj, conv_st, w["conv_w"])
    chunk = GDN_CHUNK if l >= GDN_CHUNK else _round_up(l, SUBLANES)
    lp = _round_up(l, chunk)
    beta, gc = gdn_gates(ba, w["a_log"], w["dt_bias"], chunk, lp)
    gct = jnp.swapaxes(gc, 1, 2).reshape(b, GDN_HV, lp // chunk, chunk)
    if l >= GDN_CHUNK:
        lb, hb = _pick(lp, (512, 256, 128, 64)), 1
    else:
        lb, hb = lp, _pick(GDN_HV, (8, 4, 2, 1))
    o, s_new = gdn_delta(qkv, proj, beta, gc, gct, w["norm_w"], ssm_st, l, chunk, lb, hb)
    x2 = matmul(o.reshape(b * l, -1), w["w_out"], res=x2, bk=_pick(o.shape[-1], (4096,)))
    return x2.reshape(b, l, d), new_conv, s_new


def _rwkv_layer(x, shift_st, wkv_st, w):
    b, l, d = x.shape
    if shift_st is None:
        shift_st = jnp.zeros((b, d), F32)
    (xr, xw, xk, xv, xa, xg), new_shift = rwkv_mix(x, w["norm_mix"], shift_st, w["mix"])
    m = b * l
    flat = lambda t: t.reshape(m, d)
    r = matmul(flat(xr), w["w_r"])
    k = matmul(flat(xk), w["w_k"])
    v = matmul(flat(xv), w["w_v"])
    wl = lora(flat(xw), w["w1"], w["w2"], w["w0"], "tanh")
    al = lora(flat(xa), w["a1"], w["a2"], w["a0"], "none")
    g = lora(flat(xg), w["g1"], w["g2"], jnp.zeros((d,), F32), "sigmoid")
    chunk = RWKV_CHUNK if l >= RWKV_CHUNK else _round_up(l, SUBLANES)
    lb = _pick(l, (512, 256, 128, 64)) if l >= RWKV_CHUNK else chunk
    to3 = lambda t: t.reshape(b, l, d)
    y, s_new = rwkv_wkv(to3(r), to3(k), to3(v), to3(wl), to3(al), to3(g), w["k_k"], w["k_a"], w["r_k"],
                        w["lnx_w"], w["lnx_b"], wkv_st, chunk, lb)
    x2 = matmul(y.reshape(m, d), w["w_out"], res=x.reshape(m, d))
    return x2.reshape(b, l, d), new_shift, s_new


def _ffn(x, w):
    b, l, d = x.shape
    x2 = x.reshape(b * l, d)
    h = rmsnorm(x2, w["norm"], BF16)
    act = gate_up(h, w["w_gate"], w["w_up"])
    dff = act.shape[1]
    bk = dff // 2 if (dff // 2) % LANES == 0 else dff
    x2 = matmul(act, w["w_down"], res=x2, bn=_pick(d, (512, 256, 128)), bk=bk)
    return x2.reshape(b, l, d)


def _trunk(x, conv_st, ssm_st, shift_st, wkv_st, wts):
    x, conv_new, ssm_new = _gdn_layer(x, conv_st, ssm_st, wts["gdn"])
    x = _ffn(x, wts["ffn"][0])
    x, shift_new, wkv_new = _rwkv_layer(x, shift_st, wkv_st, wts["rwkv"])
    x = _ffn(x, wts["ffn"][1])
    b, l, d = x.shape
    y = rmsnorm(x.reshape(b * l, d), wts["norm_final"], F32).reshape(b, l, d)
    return y, conv_new[None], ssm_new[None], shift_new[None], wkv_new[None]


def _pad_to(x, axis, size):
    pad = size - x.shape[axis]
    if pad == 0:
        return x
    widths = [(0, 0)] * x.ndim
    widths[axis] = (0, pad)
    return jnp.pad(x, widths)


def kernel(x_prompt, x_sample, state_gdn_conv, state_gdn_ssm, state_rwkv_shift, state_rwkv_wkv,
           norm_mix, norm_ffn, norm_final,
           gdn_w_in, gdn_conv_w, gdn_a_log, gdn_dt_bias, gdn_norm_w, gdn_w_out,
           rwkv_mix, rwkv_w_r, rwkv_w_k, rwkv_w_v, rwkv_w0, rwkv_w1, rwkv_w2,
           rwkv_a0, rwkv_a1, rwkv_a2, rwkv_g1, rwkv_g2, rwkv_k_k, rwkv_k_a, rwkv_r_k,
           rwkv_lnx_w, rwkv_lnx_b, rwkv_w_out,
           ffn_w_gate, ffn_w_up, ffn_w_down):
    main_cols = 2 * GDN_HK * GDN_DK + 2 * GDN_HV * GDN_DV
    bf = lambda t: t.astype(BF16)
    gate_rank = _round_up(rwkv_g1.shape[-1], LANES)
    wts = {
        "gdn": {
            "norm_mix": norm_mix[0],
            "w_in_main": bf(gdn_w_in[0][:, :main_cols]),
            "w_in_ba": bf(_pad_to(gdn_w_in[0][:, main_cols:], 1, _round_up(2 * GDN_HV, LANES))),
            "conv_w": gdn_conv_w[0], "a_log": gdn_a_log[0], "dt_bias": gdn_dt_bias[0],
            "norm_w": gdn_norm_w[0], "w_out": bf(gdn_w_out[0]),
        },
        "rwkv": {
            "norm_mix": norm_mix[1], "mix": rwkv_mix[0],
            "w_r": bf(rwkv_w_r[0]), "w_k": bf(rwkv_w_k[0]), "w_v": bf(rwkv_w_v[0]),
            "w0": rwkv_w0[0], "w1": bf(rwkv_w1[0]), "w2": bf(rwkv_w2[0]),
            "a0": rwkv_a0[0], "a1": bf(rwkv_a1[0]), "a2": bf(rwkv_a2[0]),
            "g1": bf(_pad_to(rwkv_g1[0], 1, gate_rank)), "g2": bf(_pad_to(rwkv_g2[0], 0, gate_rank)),
            "k_k": rwkv_k_k[0], "k_a": rwkv_k_a[0], "r_k": rwkv_r_k[0],
            "lnx_w": rwkv_lnx_w[0], "lnx_b": rwkv_lnx_b[0], "w_out": bf(rwkv_w_out[0]),
        },
        "ffn": [{"norm": norm_ffn[i], "w_gate": bf(ffn_w_gate[i]), "w_up": bf(ffn_w_up[i]),
                 "w_down": bf(ffn_w_down[i])} for i in range(DEPTH)],
        "norm_final": norm_final,
    }
    outs_p = _trunk(x_prompt, None, None, None, None, wts)
    outs_s = _trunk(x_sample, state_gdn_conv[0], state_gdn_ssm[0], state_rwkv_shift[0], state_rwkv_wkv[0], wts)
    return (outs_p[0], outs_s[0]) + tuple(outs_p[1:]) + tuple(outs_s[1:])
```

```python
import functools

import jax
import jax.numpy as jnp
from jax import lax
from jax.experimental import pallas as pl
from jax.experimental.pallas import tpu as pltpu

F32 = jnp.float32
BF16 = jnp.bfloat16
HIGHEST = lax.Precision.HIGHEST

D_MODEL = 4096
DEPTH = 2
GDN_HK = 32
GDN_HV = 64
GDN_DK = 128
GDN_DV = 128
GDN_CONV_W = 4
GDN_CHUNK = 64
RWKV_N = 64
RWKV_CHUNK = 64
RWKV_LNX_EPS = 64e-5
RMS_EPS = 1e-6
L2_EPS = 1e-6

LANES = 128
SUBLANES = 8
VMEM_LIMIT = 56 * 1024 * 1024


def _cparams(sem):
    return pltpu.CompilerParams(dimension_semantics=sem, vmem_limit_bytes=VMEM_LIMIT)


def _pick(n, prefs):
    for p in prefs:
        if n % p == 0:
            return p
    return n


def _dot(a, b, precision=None):
    return jnp.dot(a, b, preferred_element_type=F32, precision=precision)


def _dot_nt(a, b, precision=None):
    return lax.dot_general(a, b, (((1,), (1,)), ((), ())), preferred_element_type=F32, precision=precision)


def _dot_tn(a, b, precision=None):
    return lax.dot_general(a, b, (((0,), (0,)), ((), ())), preferred_element_type=F32, precision=precision)


def _sigmoid(x):
    return 1.0 / (1.0 + jnp.exp(-x))


def _softplus(x):
    return jnp.maximum(x, 0.0) + jnp.log1p(jnp.exp(-jnp.abs(x)))


def _tri_masks(c):
    ii = lax.broadcasted_iota(jnp.int32, (c, c), 0)
    jj = lax.broadcasted_iota(jnp.int32, (c, c), 1)
    return ii >= jj, ii > jj, ii == jj


def _bdot(a, b):
    return lax.dot_general(a, b, (((2,), (1,)), ((0,), (0,))), preferred_element_type=F32)


def _bdot_nt(a, b):
    return lax.dot_general(a, b, (((2,), (2,)), ((0,), (0,))), preferred_element_type=F32)


def _bdot_tn(a, b):
    return lax.dot_general(a, b, (((1,), (1,)), ((0,), (0,))), preferred_element_type=F32)


def _unit_lower_inverse(m, c):
    _, _, diag = _tri_masks(c)
    x = jnp.where(diag, 1.0, 0.0).astype(F32)[None] - m
    p = m
    power = 1
    while 2 * power < c:
        p = _bdot(p, p)
        power *= 2
        x = x + _bdot(x, p)
    return x


def _chunk_cumsum(x, c):
    incl, _, _ = _tri_masks(c)
    tri = jnp.broadcast_to(jnp.where(incl, 1.0, 0.0).astype(BF16)[None], (x.shape[0], c, c))
    hi = x.astype(BF16)
    rest = x - hi.astype(F32)
    mid = rest.astype(BF16)
    lo = (rest - mid.astype(F32)).astype(BF16)
    return _bdot(tri, hi) + _bdot(tri, mid) + _bdot(tri, lo)


def _take(x, idx):
    return jnp.stack([x[i] for i in idx])


def _mm_kernel(*refs, nk, has_res):
    if has_res:
        a_ref, b_ref, r_ref, o_ref = refs[:4]
        scratch = refs[4:]
    else:
        a_ref, b_ref, o_ref = refs[:3]
        r_ref = None
        scratch = refs[3:]
    part = _dot(a_ref[...], b_ref[...])

    def finish(total):
        if has_res:
            total = total + r_ref[...]
        o_ref[...] = total.astype(o_ref.dtype)

    if nk == 1:
        finish(part)
        return
    acc_ref = scratch[0]
    k = pl.program_id(2)

    @pl.when(k == 0)
    def _():
        acc_ref[...] = part

    @pl.when(jnp.logical_and(k > 0, k < nk - 1))
    def _():
        acc_ref[...] += part

    @pl.when(k == nk - 1)
    def _():
        finish(acc_ref[...] + part)


def matmul(a, b, res=None, out_dtype=F32, bm=None, bn=None, bk=None):
    m, kdim = a.shape
    _, n = b.shape
    bm = bm or _pick(m, (1024, 512, 256, 128, 64, 32, 16, 8))
    bk = bk or kdim
    bn = bn or _pick(n, (1024, 512, 256, 128) if (res is None and bk == kdim) else (512, 256, 128))
    nk = kdim // bk
    in_specs = [pl.BlockSpec((bm, bk), lambda i, j, k: (i, k)),
                pl.BlockSpec((bk, bn), lambda i, j, k: (k, j))]
    args = [a, b]
    if res is not None:
        in_specs.append(pl.BlockSpec((bm, bn), lambda i, j, k: (i, j)))
        args.append(res)
    scratch = [pltpu.VMEM((bm, bn), F32)] if nk > 1 else []
    return pl.pallas_call(
        functools.partial(_mm_kernel, nk=nk, has_res=res is not None),
        grid=(m // bm, n // bn, nk),
        in_specs=in_specs,
        out_specs=pl.BlockSpec((bm, bn), lambda i, j, k: (i, j)),
        out_shape=jax.ShapeDtypeStruct((m, n), out_dtype),
        scratch_shapes=scratch,
        compiler_params=_cparams(("parallel", "parallel", "arbitrary")),
        name="matmul",
    )(*args)


def _gate_up_kernel(h_ref, wg_ref, wu_ref, o_ref):
    h = h_ref[...]
    g = _dot(h, wg_ref[...])
    u = _dot(h, wu_ref[...])
    o_ref[...] = (g * _sigmoid(g) * u).astype(o_ref.dtype)


def gate_up(h, wg, wu):
    m, kdim = h.shape
    n = wg.shape[1]
    bm = _pick(m, (1024, 512, 256, 128, 64, 32, 16, 8))
    bn = _pick(n, (512, 256, 128))
    return pl.pallas_call(
        _gate_up_kernel,
        grid=(m // bm, n // bn),
        in_specs=[pl.BlockSpec((bm, kdim), lambda i, j: (i, 0)),
                  pl.BlockSpec((kdim, bn), lambda i, j: (0, j)),
                  pl.BlockSpec((kdim, bn), lambda i, j: (0, j))],
        out_specs=pl.BlockSpec((bm, bn), lambda i, j: (i, j)),
        out_shape=jax.ShapeDtypeStruct((m, n), BF16),
        compiler_params=_cparams(("parallel", "parallel")),
        name="ffn_gate_up",
    )(h, wg, wu)


def _rms_kernel(x_ref, w_ref, o_ref):
    x = x_ref[...]
    y = x * lax.rsqrt(jnp.mean(x * x, axis=-1, keepdims=True) + RMS_EPS)
    o_ref[...] = (y * w_ref[...]).astype(o_ref.dtype)


def rmsnorm(x, w, out_dtype):
    m, d = x.shape
    br = _pick(m, (256, 128, 64, 32, 16, 8))
    return pl.pallas_call(
        _rms_kernel,
        grid=(m // br,),
        in_specs=[pl.BlockSpec((br, d), lambda i: (i, 0)),
                  pl.BlockSpec((1, d), lambda i: (0, 0))],
        out_specs=pl.BlockSpec((br, d), lambda i: (i, 0)),
        out_shape=jax.ShapeDtypeStruct((m, d), out_dtype),
        compiler_params=_cparams(("parallel",)),
        name="rmsnorm",
    )(x, w.reshape(1, d))


def _gdn_prep_kernel(x_ref, buf_ref, cw_ref, o_ref, win_ref, *, lb, n_q_blocks, n_qk_blocks, q_scale):
    cblk = pl.program_id(1)
    l = pl.program_id(2)
    halo = GDN_CONV_W - 1
    top = SUBLANES

    @pl.when(l == 0)
    def _():
        win_ref[top - halo:top, :] = buf_ref[0]

    win_ref[top:top + lb, :] = x_ref[0]
    acc = win_ref[top - halo:top - halo + lb, :] * cw_ref[0:1, :]
    for j in range(1, GDN_CONV_W):
        acc = acc + win_ref[top - halo + j:top - halo + j + lb, :] * cw_ref[j:j + 1, :]
    carry = win_ref[top + lb - halo:top + lb, :]
    win_ref[top - halo:top, :] = carry
    y = acc * _sigmoid(acc)

    @pl.when(cblk >= n_qk_blocks)
    def _():
        o_ref[0] = y

    @pl.when(cblk < n_qk_blocks)
    def _():
        scale = jnp.where(cblk < n_q_blocks, q_scale, 1.0).astype(F32)
        for s in range(y.shape[1] // GDN_DK):
            ys = y[:, s * GDN_DK:(s + 1) * GDN_DK]
            inv = lax.rsqrt(jnp.sum(ys * ys, axis=-1, keepdims=True) + L2_EPS) * scale
            o_ref[0, :, s * GDN_DK:(s + 1) * GDN_DK] = ys * inv


def gdn_prep(proj, conv_buf, conv_w):
    b, l, _ = proj.shape
    kdim = GDN_HK * GDN_DK
    conv_dim = conv_w.shape[1]
    cb = _pick(kdim, (2048, 1024, 512, 256, 128))
    lb = _pick(l, (256, 128, 64, 32, 16, 8))
    kern = functools.partial(_gdn_prep_kernel, lb=lb, n_q_blocks=kdim // cb, n_qk_blocks=2 * kdim // cb,
                             q_scale=float(GDN_DK) ** -0.5)
    return pl.pallas_call(
        kern,
        grid=(b, conv_dim // cb, l // lb),
        in_specs=[pl.BlockSpec((1, lb, cb), lambda i, c, t: (i, t, c)),
                  pl.BlockSpec((1, GDN_CONV_W - 1, cb), lambda i, c, t: (i, 0, c)),
                  pl.BlockSpec((GDN_CONV_W, cb), lambda i, c, t: (0, c))],
        out_specs=pl.BlockSpec((1, lb, cb), lambda i, c, t: (i, t, c)),
        out_shape=jax.ShapeDtypeStruct((b, l, conv_dim), F32),
        scratch_shapes=[pltpu.VMEM((SUBLANES + lb, cb), F32)],
        compiler_params=_cparams(("parallel", "parallel", "arbitrary")),
        name="gdn_prep",
    )(proj, conv_buf, conv_w)


def _gdn_gates_kernel(ba_ref, alog_ref, dtb_ref, beta_ref, gc_ref, pad_ref, *, lb, lpb, chunk):
    hv = GDN_HV
    ba = ba_ref[0]
    beta = _sigmoid(ba[:, :hv])
    g = -jnp.exp(alog_ref[...]) * _softplus(ba[:, hv:2 * hv] + dtb_ref[...])
    if lpb != lb:
        pad_ref[...] = jnp.zeros_like(pad_ref)
        pad_ref[0:lb, :] = g
        g = pad_ref[...]
        pad_ref[0:lb, :] = beta
        beta = pad_ref[...]
    ii = lax.broadcasted_iota(jnp.int32, (lpb, lpb), 0)
    jj = lax.broadcasted_iota(jnp.int32, (lpb, lpb), 1)
    same_chunk_causal = jnp.logical_and(ii >= jj, ii // chunk == jj // chunk)
    gc = _dot(jnp.where(same_chunk_causal, 1.0, 0.0).astype(F32), g, HIGHEST)
    beta_ref[0] = beta
    gc_ref[0] = gc


def gdn_gates(ba, a_log, dt_bias, chunk, lp):
    b, l, w = ba.shape
    lb = _pick(l, (256, 128, 64, 32, 16, 8))
    lpb = lb if lp == l else lp
    kern = functools.partial(_gdn_gates_kernel, lb=lb, lpb=lpb, chunk=chunk)
    out = jax.ShapeDtypeStruct((b, lp, GDN_HV), F32)
    return pl.pallas_call(
        kern,
        grid=(b, l // lb),
        in_specs=[pl.BlockSpec((1, lb, w), lambda i, t: (i, t, 0)),
                  pl.BlockSpec((1, GDN_HV), lambda i, t: (0, 0)),
                  pl.BlockSpec((1, GDN_HV), lambda i, t: (0, 0))],
        out_specs=[pl.BlockSpec((1, lpb, GDN_HV), lambda i, t: (i, t, 0)),
                   pl.BlockSpec((1, lpb, GDN_HV), lambda i, t: (i, t, 0))],
        out_shape=[out, out],
        scratch_shapes=[pltpu.VMEM((lpb, GDN_HV), F32)],
        compiler_params=_cparams(("parallel", "parallel")),
        name="gdn_gates",
    )(ba, a_log.reshape(1, GDN_HV), dt_bias.reshape(1, GDN_HV))


def _gdn_delta_kernel(*refs, l_real, lb, chunk, hb, has_s0):
    if has_s0:
        q_ref, k_ref, v_ref, z_ref, beta_ref, gc_ref, gct_ref, nw_ref, s0_ref = refs[:9]
        rest = refs[9:]
    else:
        q_ref, k_ref, v_ref, z_ref, beta_ref, gc_ref, gct_ref, nw_ref = refs[:8]
        s0_ref = None
        rest = refs[8:]
    o_ref, so_ref, s_scr, padk_scr, padv_scr = rest
    hblk = pl.program_id(1)
    t = pl.program_id(2)
    nt = pl.num_programs(2)
    c = chunk
    nc = lb // c
    padded = l_real < lb
    rep = GDN_HV // GDN_HK
    kh = hb // rep
    incl, strict, _ = _tri_masks(c)

    @pl.when(t == 0)
    def _():
        if has_s0:
            s_scr[...] = s0_ref[0]
        else:
            s_scr[...] = jnp.zeros_like(s_scr)

    def load(ref, pad_scr):
        if padded:
            pad_scr[...] = jnp.zeros_like(pad_scr)
            pad_scr[0:l_real, :] = ref[0]
            return pad_scr[...]
        return ref[0]

    q2 = load(q_ref, padk_scr)
    k2 = load(k_ref, padk_scr)
    v2 = load(v_ref, padv_scr)
    beta_t = beta_ref[0]
    gc_t = gc_ref[0]
    lane = lax.broadcasted_iota(jnp.int32, (lb, GDN_HV), 1)

    q3, k3, qk, kk = [], [], [], []
    for p in range(kh):
        qp = q2[:, p * GDN_DK:(p + 1) * GDN_DK].reshape(nc, c, GDN_DK)
        kp = k2[:, p * GDN_DK:(p + 1) * GDN_DK].reshape(nc, c, GDN_DK)
        q3.append(qp)
        k3.append(kp)
        qk.append(_bdot_nt(qp, kp))
        kk.append(_bdot_nt(kp, kp))

    m_l, rhs_l, qg_l, kg_l, gl_l, ai_l = [], [], [], [], [], []
    for j in range(hb):
        sel = lane == hblk * hb + j
        beta_col = jnp.sum(jnp.where(sel, beta_t, 0.0), axis=-1, keepdims=True).reshape(nc, c, 1)
        gc_col = jnp.sum(jnp.where(sel, gc_t, 0.0), axis=-1, keepdims=True).reshape(nc, c, 1)
        gc_row = gct_ref[0, j]
        decay = jnp.where(incl, jnp.exp(jnp.where(incl, gc_col - gc_row, 0.0)), 0.0)
        eg = jnp.exp(gc_col)
        g_last = gc_col[:, c - 1:c, :]
        kp = k3[j // rep]
        vj = v2[:, j * GDN_DV:(j + 1) * GDN_DV].reshape(nc, c, GDN_DV)
        ai_l.append(qk[j // rep] * decay)
        m_l.append(jnp.where(strict, kk[j // rep] * (beta_col * decay), 0.0))
        rhs_l.append(jnp.concatenate([vj * beta_col, kp * (beta_col * eg)], axis=-1))
        qg_l.append(q3[j // rep] * eg)
        kg_l.append(kp * jnp.exp(g_last - gc_col))
        gl_l.append(jnp.exp(g_last))
    cat = lambda xs: jnp.concatenate(xs, axis=0)
    tinv = _unit_lower_inverse(cat(m_l), c)
    uw = _bdot(tinv, cat(rhs_l))
    u = uw[:, :, :GDN_DV]
    w = uw[:, :, GDN_DV:]
    qg, kg, gl, a_intra = cat(qg_l), cat(kg_l), cat(gl_l), cat(ai_l)

    s = s_scr[...]
    for ci in range(nc):
        idx = [j * nc + ci for j in range(hb)]
        ws_qs = _bdot(jnp.concatenate([_take(w, idx), _take(qg, idx)], axis=1), s)
        v_new = _take(u, idx) - ws_qs[:, :c]
        o = ws_qs[:, c:] + _bdot(_take(a_intra, idx), v_new)
        s = s * _take(gl, idx) + _bdot_tn(_take(kg, idx), v_new)
        o = o * lax.rsqrt(jnp.mean(o * o, axis=-1, keepdims=True) + RMS_EPS) * nw_ref[...]
        for j in range(hb):
            vslab = slice(j * GDN_DV, (j + 1) * GDN_DV)
            if padded:
                z = z_ref[0, :, vslab]
                o_ref[0, :, vslab] = (o[j][0:l_real] * (z * _sigmoid(z))).astype(o_ref.dtype)
            else:
                z = z_ref[0, ci * c:(ci + 1) * c, vslab]
                o_ref[0, ci * c:(ci + 1) * c, vslab] = (o[j] * (z * _sigmoid(z))).astype(o_ref.dtype)
    s_scr[...] = s

    @pl.when(t == nt - 1)
    def _():
        so_ref[0] = s


def gdn_delta(qkv, proj, beta, gc, gct, norm_w, s0, l_real, chunk, lb, hb):
    b = qkv.shape[0]
    conv_dim = 2 * GDN_HK * GDN_DK + GDN_HV * GDN_DV
    lp = beta.shape[1]
    lrows = l_real if l_real < lb else lb
    nt = lp // lb
    rep = GDN_HV // GDN_HK
    kh = hb // rep
    k_base = GDN_HK // kh
    v_base = (2 * GDN_HK * GDN_DK) // (hb * GDN_DV)
    z_base = conv_dim // (hb * GDN_DV)
    in_specs = [
        pl.BlockSpec((1, lrows, kh * GDN_DK), lambda i, h, t: (i, t, h)),
        pl.BlockSpec((1, lrows, kh * GDN_DK), lambda i, h, t: (i, t, k_base + h)),
        pl.BlockSpec((1, lrows, hb * GDN_DV), lambda i, h, t: (i, t, v_base + h)),
        pl.BlockSpec((1, lrows, hb * GDN_DV), lambda i, h, t: (i, t, z_base + h)),
        pl.BlockSpec((1, lb, GDN_HV), lambda i, h, t: (i, t, 0)),
        pl.BlockSpec((1, lb, GDN_HV), lambda i, h, t: (i, t, 0)),
        pl.BlockSpec((1, hb, lb // chunk, 1, chunk), lambda i, h, t: (i, h, t, 0, 0)),
        pl.BlockSpec((1, GDN_DV), lambda i, h, t: (0, 0)),
    ]
    args = [qkv, qkv, qkv, proj, beta, gc, gct, norm_w.reshape(1, GDN_DV)]
    if s0 is not None:
        in_specs.append(pl.BlockSpec((1, hb, GDN_DK, GDN_DV), lambda i, h, t: (i, h, 0, 0)))
        args.append(s0)
    kern = functools.partial(_gdn_delta_kernel, l_real=l_real, lb=lb, chunk=chunk, hb=hb, has_s0=s0 is not None)
    return pl.pallas_call(
        kern,
        grid=(b, GDN_HV // hb, nt),
        in_specs=in_specs,
        out_specs=[pl.BlockSpec((1, lrows, hb * GDN_DV), lambda i, h, t: (i, t, h)),
                   pl.BlockSpec((1, hb, GDN_DK, GDN_DV), lambda i, h, t: (i, h, 0, 0))],
        out_shape=[jax.ShapeDtypeStruct((b, l_real, GDN_HV * GDN_DV), BF16),
                   jax.ShapeDtypeStruct((b, GDN_HV, GDN_DK, GDN_DV), F32)],
        scratch_shapes=[pltpu.VMEM((hb, GDN_DK, GDN_DV), F32),
                        pltpu.VMEM((lb, kh * GDN_DK), F32),
                        pltpu.VMEM((lb, hb * GDN_DV), F32)],
        compiler_params=_cparams(("parallel", "parallel", "arbitrary")),
        name="gdn_delta",
    )(*args)


def _rwkv_mix_kernel(x_ref, nw_ref, shift_ref, mix_ref, o0, o1, o2, o3, o4, o5, last_ref, win_ref, *, lb):
    t = pl.program_id(1)
    top = SUBLANES

    @pl.when(t == 0)
    def _():
        win_ref[top - 1:top, :] = shift_ref[0]

    x = x_ref[0]
    h = x * lax.rsqrt(jnp.mean(x * x, axis=-1, keepdims=True) + RMS_EPS) * nw_ref[...]
    win_ref[top:top + lb, :] = h
    prev = win_ref[top - 1:top - 1 + lb, :]
    last = win_ref[top + lb - 1:top + lb, :]
    win_ref[top - 1:top, :] = last
    last_ref[0] = last
    xx = prev - h
    for i, o in enumerate((o0, o1, o2, o3, o4, o5)):
        o[0] = (h + xx * mix_ref[i:i + 1, :]).astype(o.dtype)


def rwkv_mix(x, norm_w, shift, mix):
    b, l, d = x.shape
    lb = _pick(l, (128, 64, 32, 16, 8))
    kern = functools.partial(_rwkv_mix_kernel, lb=lb)
    tok = pl.BlockSpec((1, lb, d), lambda i, t: (i, t, 0))
    row = pl.BlockSpec((1, 1, d), lambda i, t: (i, 0, 0))
    outs = pl.pallas_call(
        kern,
        grid=(b, l // lb),
        in_specs=[tok, pl.BlockSpec((1, d), lambda i, t: (0, 0)), row,
                  pl.BlockSpec((6, d), lambda i, t: (0, 0))],
        out_specs=[tok] * 6 + [row],
        out_shape=[jax.ShapeDtypeStruct((b, l, d), BF16)] * 6 + [jax.ShapeDtypeStruct((b, 1, d), F32)],
        scratch_shapes=[pltpu.VMEM((SUBLANES + lb, d), F32)],
        compiler_params=_cparams(("parallel", "arbitrary")),
        name="rwkv_mix",
    )(x, norm_w.reshape(1, d), shift.reshape(b, 1, d), mix)
    return outs[:6], outs[6].reshape(b, d)


def _lora_kernel(x_ref, w1_ref, w2_ref, b_ref, o_ref, *, act):
    hmid = _dot(x_ref[...], w1_ref[...])
    if act == "tanh":
        hmid = jnp.tanh(hmid)
    elif act == "sigmoid":
        hmid = _sigmoid(hmid)
    o_ref[...] = b_ref[...] + _dot(hmid.astype(BF16), w2_ref[...])


def lora(x, w1, w2, bias, act):
    m, d = x.shape
    r = w1.shape[1]
    n = w2.shape[1]
    bm = _pick(m, (256, 128, 64, 32, 16, 8))
    return pl.pallas_call(
        functools.partial(_lora_kernel, act=act),
        grid=(m // bm,),
        in_specs=[pl.BlockSpec((bm, d), lambda i: (i, 0)),
                  pl.BlockSpec((d, r), lambda i: (0, 0)),
                  pl.BlockSpec((r, n), lambda i: (0, 0)),
                  pl.BlockSpec((1, n), lambda i: (0, 0))],
        out_specs=pl.BlockSpec((bm, n), lambda i: (i, 0)),
        out_shape=jax.ShapeDtypeStruct((m, n), F32),
        compiler_params=_cparams(("parallel",)),
        name="rwkv_lora",
    )(x, w1, w2, bias.reshape(1, n))


def _rwkv_wkv_kernel(*refs, l_real, lb, chunk, has_s0):
    n_in = 12 if has_s0 else 11
    (r_ref, k_ref, v_ref, w_ref, a_ref, g_ref, kk_ref, ka_ref, rk_ref, lnw_ref, lnb_ref) = refs[:11]
    s0_ref = refs[11] if has_s0 else None
    o_ref, so_ref, s_scr, pad_scr = refs[n_in:]
    t = pl.program_id(2)
    nt = pl.num_programs(2)
    n = RWKV_N
    c = chunk
    nc = lb // c
    width = r_ref.shape[-1]
    heads = width // n
    padded = l_real < lb
    incl, strict, _ = _tri_masks(c)

    @pl.when(t == 0)
    def _():
        if has_s0:
            s_scr[...] = s0_ref[0]
        else:
            s_scr[...] = jnp.zeros_like(s_scr)

    def load(ref):
        if padded:
            pad_scr[...] = jnp.zeros_like(pad_scr)
            pad_scr[0:l_real, :] = ref[0]
            return pad_scr[...]
        return ref[0]

    r2, k2, v2, w2, a2, g2 = (load(ref) for ref in (r_ref, k_ref, v_ref, w_ref, a_ref, g_ref))
    lw2 = -jnp.exp(-_softplus(-w2) - 0.5)
    if padded:
        rows = lax.broadcasted_iota(jnp.int32, lw2.shape, 0)
        lw2 = jnp.where(rows < l_real, lw2, 0.0)
    a_sig = _sigmoid(a2)
    kmod = k2 * (1.0 + (a_sig - 1.0) * ka_ref[...])
    cl2 = _chunk_cumsum(lw2.reshape(nc, c, width), c).reshape(lb, width)
    ep2 = jnp.exp(cl2)
    em2 = jnp.exp(-cl2)

    def split(x2):
        x3 = x2.reshape(nc, c, width)
        return jnp.concatenate([x3[:, :, h * n:(h + 1) * n] for h in range(heads)], axis=0)

    def per_head(p_ref):
        return jnp.concatenate([jnp.broadcast_to(p_ref[:, h * n:(h + 1) * n][None], (nc, 1, n))
                                for h in range(heads)], axis=0)

    kk = split(k2 * kk_ref[...])
    kk = kk * lax.rsqrt(jnp.sum(kk * kk, axis=-1, keepdims=True) + L2_EPS)
    em = split(em2)
    ep = split(ep2)
    a_h = -kk * split(jnp.exp(cl2 - lw2))
    b_h = kk * split(a_sig) * em
    r = split(r2)
    k = split(kmod)
    v = split(v2)
    r_h = r * ep
    k_h = k * em
    pc = ep[:, c - 1:c, :]
    a_ak = jnp.where(strict, _bdot_nt(a_h, k_h), 0.0)
    a_ab = jnp.where(strict, _bdot_nt(a_h, b_h), 0.0)
    a_rk = jnp.where(incl, _bdot_nt(r_h, k_h), 0.0)
    a_rb = jnp.where(incl, _bdot_nt(r_h, b_h), 0.0)
    tinv = _unit_lower_inverse(-a_ab, c)
    wmat = _bdot(tinv, a_h)
    ut = _bdot(tinv, _bdot(a_ak, v))
    rt = r_h + _bdot(a_rb, wmat)
    yt = _bdot(a_rb, ut) + _bdot(a_rk, v)
    bt = b_h * pc
    hmat = _bdot_tn(ut, bt) + _bdot_tn(v, k_h * pc)

    s = s_scr[...]
    ys = [None] * (heads * nc)
    for ci in range(nc):
        idx = [h * nc + ci for h in range(heads)]
        y_c = _bdot_nt(_take(rt, idx), s) + _take(yt, idx)
        sw = _bdot_nt(s, _take(wmat, idx))
        s = s * _take(pc, idx) + _bdot(sw, _take(bt, idx)) + _take(hmat, idx)
        for j, i in enumerate(idx):
            ys[i] = y_c[j]
    s_scr[...] = s
    y = jnp.stack(ys)

    mu = jnp.mean(y, axis=-1, keepdims=True)
    yc = y - mu
    var = jnp.mean(yc * yc, axis=-1, keepdims=True)
    yn = yc * lax.rsqrt(var + RWKV_LNX_EPS) * per_head(lnw_ref) + per_head(lnb_ref)
    yn = yn + jnp.sum(r * k * per_head(rk_ref), axis=-1, keepdims=True) * v
    out = (yn * split(g2)).astype(o_ref.dtype)
    for h in range(heads):
        rows = out[h * nc:(h + 1) * nc].reshape(lb, n)
        if padded:
            o_ref[0, :, h * n:(h + 1) * n] = rows[0:l_real]
        else:
            o_ref[0, :, h * n:(h + 1) * n] = rows

    @pl.when(t == nt - 1)
    def _():
        so_ref[0] = s


def rwkv_wkv(r, k, v, w, a, g, k_k, k_a, r_k, lnx_w, lnx_b, s0, chunk, lb, heads):
    b, l, d = r.shape
    n = RWKV_N
    h = d // n
    width = heads * n
    lrows = l if l < lb else lb
    nt = max(1, l // lb)
    tok = pl.BlockSpec((1, lrows, width), lambda i, p, t: (i, t, p))
    par = pl.BlockSpec((1, width), lambda i, p, t: (0, p))
    st = pl.BlockSpec((1, heads, n, n), lambda i, p, t: (i, p, 0, 0))
    in_specs = [tok] * 6 + [par] * 5
    args = [r, k, v, w, a, g, k_k.reshape(1, d), k_a.reshape(1, d), r_k.reshape(1, d),
            lnx_w.reshape(1, d), lnx_b.reshape(1, d)]
    if s0 is not None:
        in_specs.append(st)
        args.append(s0)
    kern = functools.partial(_rwkv_wkv_kernel, l_real=l, lb=lb, chunk=chunk, has_s0=s0 is not None)
    return pl.pallas_call(
        kern,
        grid=(b, h // heads, nt),
        in_specs=in_specs,
        out_specs=[tok, st],
        out_shape=[jax.ShapeDtypeStruct((b, l, d), BF16),
                   jax.ShapeDtypeStruct((b, h, n, n), F32)],
        scratch_shapes=[pltpu.VMEM((heads, n, n), F32), pltpu.VMEM((lb, width), F32)],
        compiler_params=_cparams(("parallel", "parallel", "arbitrary")),
        name="rwkv_wkv",
    )(*args)


def _round_up(x, m):
    return -(-x // m) * m


def _gdn_layer(x, conv_st, ssm_st, w):
    b, l, d = x.shape
    conv_dim = 2 * GDN_HK * GDN_DK + GDN_HV * GDN_DV
    x2 = x.reshape(b * l, d)
    h = rmsnorm(x2, w["norm_mix"], BF16)
    proj = matmul(h, w["w_in_main"]).reshape(b, l, -1)
    ba = matmul(h, w["w_in_ba"]).reshape(b, l, -1)
    if conv_st is None:
        conv_st = jnp.zeros((b, GDN_CONV_W - 1, conv_dim), F32)
    new_conv = proj[:, l - (GDN_CONV_W - 1):, :conv_dim]
    qkv = gdn_prep(proj, conv_st, w["conv_w"])
    chunk = GDN_CHUNK if l >= GDN_CHUNK else _round_up(l, SUBLANES)
    lp = _round_up(l, chunk)
    beta, gc = gdn_gates(ba, w["a_log"], w["dt_bias"], chunk, lp)
    gct = jnp.swapaxes(gc, 1, 2).reshape(b, GDN_HV, lp // chunk, 1, chunk)
    if l >= GDN_CHUNK:
        lb, hb = _pick(lp, (256, 128, 64)), _pick(GDN_HV, (4, 2))
    else:
        lb, hb = lp, _pick(GDN_HV, (16, 8, 4, 2))
    o, s_new = gdn_delta(qkv, proj, beta, gc, gct, w["norm_w"], ssm_st, l, chunk, lb, hb)
    x2 = matmul(o.reshape(b * l, -1), w["w_out"], res=x2, bk=_pick(o.shape[-1], (4096,)))
    return x2.reshape(b, l, d), new_conv, s_new


def _rwkv_layer(x, shift_st, wkv_st, w):
    b, l, d = x.shape
    if shift_st is None:
        shift_st = jnp.zeros((b, d), F32)
    (xr, xw, xk, xv, xa, xg), new_shift = rwkv_mix(x, w["norm_mix"], shift_st, w["mix"])
    m = b * l
    flat = lambda t: t.reshape(m, d)
    r = matmul(flat(xr), w["w_r"])
    k = matmul(flat(xk), w["w_k"])
    v = matmul(flat(xv), w["w_v"])
    wl = lora(flat(xw), w["w1"], w["w2"], w["w0"], "tanh")
    al = lora(flat(xa), w["a1"], w["a2"], w["a0"], "none")
    g = lora(flat(xg), w["g1"], w["g2"], jnp.zeros((d,), F32), "sigmoid")
    chunk = RWKV_CHUNK if l >= RWKV_CHUNK else _round_up(l, SUBLANES)
    n_heads = d // RWKV_N
    if l >= RWKV_CHUNK:
        lb, heads = _pick(l, (256, 128, 64)), _pick(n_heads, (4, 2))
    else:
        lb, heads = chunk, _pick(n_heads, (32, 16, 8, 4, 2))
    to3 = lambda t: t.reshape(b, l, d)
    y, s_new = rwkv_wkv(to3(r), to3(k), to3(v), to3(wl), to3(al), to3(g), w["k_k"], w["k_a"], w["r_k"],
                        w["lnx_w"], w["lnx_b"], wkv_st, chunk, lb, heads)
    x2 = matmul(y.reshape(m, d), w["w_out"], res=x.reshape(m, d))
    return x2.reshape(b, l, d), new_shift, s_new


def _ffn(x, w):
    b, l, d = x.shape
    x2 = x.reshape(b * l, d)
    h = rmsnorm(x2, w["norm"], BF16)
    act = gate_up(h, w["w_gate"], w["w_up"])
    dff = act.shape[1]
    bk = dff // 2 if (dff // 2) % LANES == 0 else dff
    x2 = matmul(act, w["w_down"], res=x2, bn=_pick(d, (512, 256, 128)), bk=bk)
    return x2.reshape(b, l, d)


def _trunk(x, conv_st, ssm_st, shift_st, wkv_st, wts):
    x, conv_new, ssm_new = _gdn_layer(x, conv_st, ssm_st, wts["gdn"])
    x = _ffn(x, wts["ffn"][0])
    x, shift_new, wkv_new = _rwkv_layer(x, shift_st, wkv_st, wts["rwkv"])
    x = _ffn(x, wts["ffn"][1])
    b, l, d = x.shape
    y = rmsnorm(x.reshape(b * l, d), wts["norm_final"], F32).reshape(b, l, d)
    return y, conv_new[None], ssm_new[None], shift_new[None], wkv_new[None]


def _pad_to(x, axis, size):
    pad = size - x.shape[axis]
    if pad == 0:
        return x
    widths = [(0, 0)] * x.ndim
    widths[axis] = (0, pad)
    return jnp.pad(x, widths)


def kernel(x_prompt, x_sample, state_gdn_conv, state_gdn_ssm, state_rwkv_shift, state_rwkv_wkv,
           norm_mix, norm_ffn, norm_final,
           gdn_w_in, gdn_conv_w, gdn_a_log, gdn_dt_bias, gdn_norm_w, gdn_w_out,
           rwkv_mix, rwkv_w_r, rwkv_w_k, rwkv_w_v, rwkv_w0, rwkv_w1, rwkv_w2,
           rwkv_a0, rwkv_a1, rwkv_a2, rwkv_g1, rwkv_g2, rwkv_k_k, rwkv_k_a, rwkv_r_k,
           rwkv_lnx_w, rwkv_lnx_b, rwkv_w_out,
           ffn_w_gate, ffn_w_up, ffn_w_down):
    main_cols = 2 * GDN_HK * GDN_DK + 2 * GDN_HV * GDN_DV
    bf = lambda t: t.astype(BF16)
    gate_rank = _round_up(rwkv_g1.shape[-1], LANES)
    wts = {
        "gdn": {
            "norm_mix": norm_mix[0],
            "w_in_main": bf(gdn_w_in[0][:, :main_cols]),
            "w_in_ba": bf(_pad_to(gdn_w_in[0][:, main_cols:], 1, _round_up(2 * GDN_HV, LANES))),
            "conv_w": gdn_conv_w[0], "a_log": gdn_a_log[0], "dt_bias": gdn_dt_bias[0],
            "norm_w": gdn_norm_w[0], "w_out": bf(gdn_w_out[0]),
        },
        "rwkv": {
            "norm_mix": norm_mix[1], "mix": rwkv_mix[0],
            "w_r": bf(rwkv_w_r[0]), "w_k": bf(rwkv_w_k[0]), "w_v": bf(rwkv_w_v[0]),
            "w0": rwkv_w0[0], "w1": bf(rwkv_w1[0]), "w2": bf(rwkv_w2[0]),
            "a0": rwkv_a0[0], "a1": bf(rwkv_a1[0]), "a2": bf(rwkv_a2[0]),
            "g1": bf(_pad_to(rwkv_g1[0], 1, gate_rank)), "g2": bf(_pad_to(rwkv_g2[0], 0, gate_rank)),
            "k_k": rwkv_k_k[0], "k_a": rwkv_k_a[0], "r_k": rwkv_r_k[0],
            "lnx_w": rwkv_lnx_w[0], "lnx_b": rwkv_lnx_b[0], "w_out": bf(rwkv_w_out[0]),
        },
        "ffn": [{"norm": norm_ffn[i], "w_gate": bf(ffn_w_gate[i]), "w_up": bf(ffn_w_up[i]),
                 "w_down": bf(ffn_w_down[i])} for i in range(DEPTH)],
        "norm_final": norm_final,
    }
    outs_p = _trunk(x_prompt, None, None, None, None, wts)
    outs_s = _trunk(x_sample, state_gdn_conv[0], state_gdn_ssm[0], state_rwkv_shift[0], state_rwkv_wkv[0], wts)
    return (outs_p[0], outs_s[0]) + tuple(outs_p[1:]) + tuple(outs_s[1:])
```

```python
import functools

import jax
import jax.numpy as jnp
from jax import lax
from jax.experimental import pallas as pl
from jax.experimental.pallas import tpu as pltpu

F32 = jnp.float32
BF16 = jnp.bfloat16
HIGHEST = lax.Precision.HIGHEST

D_MODEL = 4096
DEPTH = 2
GDN_HK = 32
GDN_HV = 64
GDN_DK = 128
GDN_DV = 128
GDN_CONV_W = 4
GDN_CHUNK = 64
RWKV_N = 64
RWKV_CHUNK = 64
RWKV_LNX_EPS = 64e-5
RMS_EPS = 1e-6
L2_EPS = 1e-6

LANES = 128
SUBLANES = 8
VMEM_LIMIT = 56 * 1024 * 1024


def _cparams(sem):
    return pltpu.CompilerParams(dimension_semantics=sem, vmem_limit_bytes=VMEM_LIMIT)


def _pick(n, prefs):
    for p in prefs:
        if n % p == 0:
            return p
    return n


def _dot(a, b, precision=None):
    return jnp.dot(a, b, preferred_element_type=F32, precision=precision)


def _dot_nt(a, b, precision=None):
    return lax.dot_general(a, b, (((1,), (1,)), ((), ())), preferred_element_type=F32, precision=precision)


def _dot_tn(a, b, precision=None):
    return lax.dot_general(a, b, (((0,), (0,)), ((), ())), preferred_element_type=F32, precision=precision)


def _sigmoid(x):
    return 1.0 / (1.0 + jnp.exp(-x))


def _softplus(x):
    return jnp.maximum(x, 0.0) + jnp.log1p(jnp.exp(-jnp.abs(x)))


def _tri_masks(c):
    ii = lax.broadcasted_iota(jnp.int32, (c, c), 0)
    jj = lax.broadcasted_iota(jnp.int32, (c, c), 1)
    return ii >= jj, ii > jj, ii == jj


def _bdot(a, b):
    return lax.dot_general(a, b, (((2,), (1,)), ((0,), (0,))), preferred_element_type=F32)


def _bdot_nt(a, b):
    return lax.dot_general(a, b, (((2,), (2,)), ((0,), (0,))), preferred_element_type=F32)


def _bdot_tn(a, b):
    return lax.dot_general(a, b, (((1,), (1,)), ((0,), (0,))), preferred_element_type=F32)


def _mxu(x, rows):
    return x.astype(BF16) if rows >= 2 * SUBLANES else x


def _unit_lower_inverse(m, c):
    _, _, diag = _tri_masks(c)
    x = jnp.where(diag, 1.0, 0.0).astype(F32)[None] - m
    p = _mxu(m, c)
    power = 1
    while 2 * power < c:
        p = _mxu(_bdot(p, p), c)
        power *= 2
        x = x + _bdot(_mxu(x, c), p)
    return x


def _chunk_cumsum(x, c):
    incl, _, _ = _tri_masks(c)
    tri = jnp.broadcast_to(jnp.where(incl, 1.0, 0.0).astype(BF16)[None], (x.shape[0], c, c))
    hi = x.astype(BF16)
    rest = x - hi.astype(F32)
    mid = rest.astype(BF16)
    lo = (rest - mid.astype(F32)).astype(BF16)
    return _bdot(tri, hi) + _bdot(tri, mid) + _bdot(tri, lo)


def _take(x, idx):
    return jnp.stack([x[i] for i in idx])


def _mm_kernel(*refs, nk, has_res):
    if has_res:
        a_ref, b_ref, r_ref, o_ref = refs[:4]
        scratch = refs[4:]
    else:
        a_ref, b_ref, o_ref = refs[:3]
        r_ref = None
        scratch = refs[3:]
    part = _dot(a_ref[...], b_ref[...])

    def finish(total):
        if has_res:
            total = total + r_ref[...]
        o_ref[...] = total.astype(o_ref.dtype)

    if nk == 1:
        finish(part)
        return
    acc_ref = scratch[0]
    k = pl.program_id(2)

    @pl.when(k == 0)
    def _():
        acc_ref[...] = part

    @pl.when(jnp.logical_and(k > 0, k < nk - 1))
    def _():
        acc_ref[...] += part

    @pl.when(k == nk - 1)
    def _():
        finish(acc_ref[...] + part)


def matmul(a, b, layer, res=None, out_dtype=F32, bm=None, bn=None, bk=None):
    m, kdim = a.shape
    n = b.shape[2]
    bm = bm or _pick(m, (1024, 512, 256, 128, 64, 32, 16, 8))
    bk = bk or kdim
    bn = bn or _pick(n, (1024, 512, 256, 128) if (res is None and bk == kdim) else (512, 256, 128))
    nk = kdim // bk
    in_specs = [pl.BlockSpec((bm, bk), lambda i, j, k: (i, k)),
                pl.BlockSpec((None, bk, bn), lambda i, j, k: (layer, k, j))]
    args = [a, b]
    if res is not None:
        in_specs.append(pl.BlockSpec((bm, bn), lambda i, j, k: (i, j)))
        args.append(res)
    scratch = [pltpu.VMEM((bm, bn), F32)] if nk > 1 else []
    return pl.pallas_call(
        functools.partial(_mm_kernel, nk=nk, has_res=res is not None),
        grid=(m // bm, n // bn, nk),
        in_specs=in_specs,
        out_specs=pl.BlockSpec((bm, bn), lambda i, j, k: (i, j)),
        out_shape=jax.ShapeDtypeStruct((m, n), out_dtype),
        scratch_shapes=scratch,
        compiler_params=_cparams(("parallel", "parallel", "arbitrary")),
        name="matmul",
    )(*args)


def _mm_ws_kernel(*refs, has_res):
    if has_res:
        a_ref, w_ref, r_ref, o_ref, wb_ref = refs
    else:
        a_ref, w_ref, o_ref, wb_ref = refs
        r_ref = None

    @pl.when(pl.program_id(1) == 0)
    def _():
        wb_ref[...] = w_ref[...].astype(BF16)

    total = _dot(a_ref[...], wb_ref[...])
    if has_res:
        total = total + r_ref[...]
    o_ref[...] = total.astype(o_ref.dtype)


def matmul_ws(a, w, layer, n_out=None, col0=0, res=None, out_dtype=F32, bn=None):
    m, kdim = a.shape
    n_out = n_out or w.shape[2]
    bm = _pick(m, (1024, 512, 256, 128, 64, 32, 16, 8))
    bn = bn or _pick(n_out, (512, 256, 128))
    assert col0 % bn == 0 and n_out % bn == 0
    off = col0 // bn
    in_specs = [pl.BlockSpec((bm, kdim), lambda j, i: (i, 0)),
                pl.BlockSpec((None, kdim, bn), lambda j, i: (layer, 0, off + j))]
    args = [a, w]
    if res is not None:
        in_specs.append(pl.BlockSpec((bm, bn), lambda j, i: (i, j)))
        args.append(res)
    return pl.pallas_call(
        functools.partial(_mm_ws_kernel, has_res=res is not None),
        grid=(n_out // bn, m // bm),
        in_specs=in_specs,
        out_specs=pl.BlockSpec((bm, bn), lambda j, i: (i, j)),
        out_shape=jax.ShapeDtypeStruct((m, n_out), out_dtype),
        scratch_shapes=[pltpu.VMEM((kdim, bn), BF16)],
        compiler_params=_cparams(("arbitrary", "arbitrary")),
        name="matmul_ws",
    )(*args)


def _gate_up_kernel(h_ref, wg_ref, wu_ref, o_ref, wgb_ref, wub_ref):
    @pl.when(pl.program_id(1) == 0)
    def _():
        wgb_ref[...] = wg_ref[...].astype(BF16)
        wub_ref[...] = wu_ref[...].astype(BF16)

    h = h_ref[...]
    g = _dot(h, wgb_ref[...])
    u = _dot(h, wub_ref[...])
    o_ref[...] = (g * _sigmoid(g) * u).astype(o_ref.dtype)


def gate_up(h, wg, wu, layer):
    m, kdim = h.shape
    n = wg.shape[2]
    bm = _pick(m, (1024, 512, 256, 128, 64, 32, 16, 8))
    bn = _pick(n, (256, 128))
    wspec = pl.BlockSpec((None, kdim, bn), lambda j, i: (layer, 0, j))
    return pl.pallas_call(
        _gate_up_kernel,
        grid=(n // bn, m // bm),
        in_specs=[pl.BlockSpec((bm, kdim), lambda j, i: (i, 0)), wspec, wspec],
        out_specs=pl.BlockSpec((bm, bn), lambda j, i: (i, j)),
        out_shape=jax.ShapeDtypeStruct((m, n), BF16),
        scratch_shapes=[pltpu.VMEM((kdim, bn), BF16), pltpu.VMEM((kdim, bn), BF16)],
        compiler_params=_cparams(("arbitrary", "arbitrary")),
        name="ffn_gate_up",
    )(h, wg, wu)


def _rms_kernel(x_ref, w_ref, o_ref):
    x = x_ref[...]
    y = x * lax.rsqrt(jnp.mean(x * x, axis=-1, keepdims=True) + RMS_EPS)
    o_ref[...] = (y * w_ref[...]).astype(o_ref.dtype)


def rmsnorm(x, w, out_dtype):
    m, d = x.shape
    br = _pick(m, (256, 128, 64, 32, 16, 8))
    return pl.pallas_call(
        _rms_kernel,
        grid=(m // br,),
        in_specs=[pl.BlockSpec((br, d), lambda i: (i, 0)),
                  pl.BlockSpec((1, d), lambda i: (0, 0))],
        out_specs=pl.BlockSpec((br, d), lambda i: (i, 0)),
        out_shape=jax.ShapeDtypeStruct((m, d), out_dtype),
        compiler_params=_cparams(("parallel",)),
        name="rmsnorm",
    )(x, w.reshape(1, d))


def _gdn_prep_kernel(x_ref, buf_ref, cw_ref, o_ref, win_ref, *, lb, n_q_blocks, n_qk_blocks, q_scale):
    cblk = pl.program_id(1)
    l = pl.program_id(2)
    halo = GDN_CONV_W - 1
    top = SUBLANES

    @pl.when(l == 0)
    def _():
        win_ref[:, top - halo:top, :] = buf_ref[...]

    win_ref[:, top:top + lb, :] = x_ref[...]
    acc = win_ref[:, top - halo:top - halo + lb, :] * cw_ref[0:1, :]
    for j in range(1, GDN_CONV_W):
        acc = acc + win_ref[:, top - halo + j:top - halo + j + lb, :] * cw_ref[j:j + 1, :]
    carry = win_ref[:, top + lb - halo:top + lb, :]
    win_ref[:, top - halo:top, :] = carry
    y = acc * _sigmoid(acc)

    @pl.when(cblk >= n_qk_blocks)
    def _():
        o_ref[...] = y

    @pl.when(cblk < n_qk_blocks)
    def _():
        scale = jnp.where(cblk < n_q_blocks, q_scale, 1.0).astype(F32)
        for s in range(y.shape[-1] // GDN_DK):
            ys = y[:, :, s * GDN_DK:(s + 1) * GDN_DK]
            inv = lax.rsqrt(jnp.sum(ys * ys, axis=-1, keepdims=True) + L2_EPS) * scale
            o_ref[:, :, s * GDN_DK:(s + 1) * GDN_DK] = ys * inv


def gdn_prep(proj, conv_buf, conv_w):
    b, l, _ = proj.shape
    kdim = GDN_HK * GDN_DK
    conv_dim = conv_w.shape[1]
    cb = _pick(kdim, (2048, 1024, 512, 256, 128))
    lb = _pick(l, (256, 128, 64, 32, 16, 8))
    bb = _pick(b, (16, 8, 4, 2, 1)) if l < SUBLANES else 1
    kern = functools.partial(_gdn_prep_kernel, lb=lb, n_q_blocks=kdim // cb, n_qk_blocks=2 * kdim // cb,
                             q_scale=float(GDN_DK) ** -0.5)
    return pl.pallas_call(
        kern,
        grid=(b // bb, conv_dim // cb, l // lb),
        in_specs=[pl.BlockSpec((bb, lb, cb), lambda i, c, t: (i, t, c)),
                  pl.BlockSpec((bb, GDN_CONV_W - 1, cb), lambda i, c, t: (i, 0, c)),
                  pl.BlockSpec((GDN_CONV_W, cb), lambda i, c, t: (0, c))],
        out_specs=pl.BlockSpec((bb, lb, cb), lambda i, c, t: (i, t, c)),
        out_shape=jax.ShapeDtypeStruct((b, l, conv_dim), F32),
        scratch_shapes=[pltpu.VMEM((bb, SUBLANES + lb, cb), F32)],
        compiler_params=_cparams(("parallel", "parallel", "arbitrary")),
        name="gdn_prep",
    )(proj, conv_buf, conv_w)


def _gdn_gates_kernel(ba_ref, alog_ref, dtb_ref, beta_ref, gc_ref, pad_ref, *, lb, lpb, chunk):
    hv = GDN_HV
    ba = ba_ref[0]
    beta = _sigmoid(ba[:, :hv])
    g = -jnp.exp(alog_ref[...]) * _softplus(ba[:, hv:2 * hv] + dtb_ref[...])
    if lpb != lb:
        pad_ref[...] = jnp.zeros_like(pad_ref)
        pad_ref[0:lb, :] = g
        g = pad_ref[...]
        pad_ref[0:lb, :] = beta
        beta = pad_ref[...]
    ii = lax.broadcasted_iota(jnp.int32, (lpb, lpb), 0)
    jj = lax.broadcasted_iota(jnp.int32, (lpb, lpb), 1)
    same_chunk_causal = jnp.logical_and(ii >= jj, ii // chunk == jj // chunk)
    gc = _dot(jnp.where(same_chunk_causal, 1.0, 0.0).astype(F32), g, HIGHEST)
    beta_ref[0] = beta
    gc_ref[0] = gc


def gdn_gates(ba, a_log, dt_bias, chunk, lp):
    b, l, w = ba.shape
    lb = _pick(l, (256, 128, 64, 32, 16, 8))
    lpb = lb if lp == l else lp
    kern = functools.partial(_gdn_gates_kernel, lb=lb, lpb=lpb, chunk=chunk)
    out = jax.ShapeDtypeStruct((b, lp, GDN_HV), F32)
    return pl.pallas_call(
        kern,
        grid=(b, l // lb),
        in_specs=[pl.BlockSpec((1, lb, w), lambda i, t: (i, t, 0)),
                  pl.BlockSpec((1, GDN_HV), lambda i, t: (0, 0)),
                  pl.BlockSpec((1, GDN_HV), lambda i, t: (0, 0))],
        out_specs=[pl.BlockSpec((1, lpb, GDN_HV), lambda i, t: (i, t, 0)),
                   pl.BlockSpec((1, lpb, GDN_HV), lambda i, t: (i, t, 0))],
        out_shape=[out, out],
        scratch_shapes=[pltpu.VMEM((lpb, GDN_HV), F32)],
        compiler_params=_cparams(("parallel", "parallel")),
        name="gdn_gates",
    )(ba, a_log.reshape(1, GDN_HV), dt_bias.reshape(1, GDN_HV))


def _gdn_delta_kernel(*refs, l_real, lb, chunk, hb, has_s0):
    if has_s0:
        q_ref, k_ref, v_ref, z_ref, beta_ref, gc_ref, gct_ref, nw_ref, s0_ref = refs[:9]
        rest = refs[9:]
    else:
        q_ref, k_ref, v_ref, z_ref, beta_ref, gc_ref, gct_ref, nw_ref = refs[:8]
        s0_ref = None
        rest = refs[8:]
    o_ref, so_ref, s_scr, padk_scr, padv_scr = rest
    hblk = pl.program_id(1)
    t = pl.program_id(2)
    nt = pl.num_programs(2)
    c = chunk
    nc = lb // c
    padded = l_real < lb
    rep = GDN_HV // GDN_HK
    kh = hb // rep
    incl, strict, _ = _tri_masks(c)

    @pl.when(t == 0)
    def _():
        if has_s0:
            s_scr[...] = s0_ref[0]
        else:
            s_scr[...] = jnp.zeros_like(s_scr)

    def load(ref, pad_scr):
        if padded:
            pad_scr[...] = jnp.zeros_like(pad_scr)
            pad_scr[0:l_real, :] = ref[0]
            return pad_scr[...]
        return ref[0]

    q2 = load(q_ref, padk_scr)
    k2 = load(k_ref, padk_scr)
    v2 = load(v_ref, padv_scr)
    beta_t = beta_ref[0]
    gc_t = gc_ref[0]
    lane = lax.broadcasted_iota(jnp.int32, (lb, GDN_HV), 1)

    q3, k3, qk, kk = [], [], [], []
    for p in range(kh):
        qp = q2[:, p * GDN_DK:(p + 1) * GDN_DK].reshape(nc, c, GDN_DK)
        kp = k2[:, p * GDN_DK:(p + 1) * GDN_DK].reshape(nc, c, GDN_DK)
        q3.append(qp)
        k3.append(kp)
        qk.append(_bdot_nt(_mxu(qp, c), _mxu(kp, c)))
        kk.append(_bdot_nt(_mxu(kp, c), _mxu(kp, c)))

    m_l, rhs_l, qg_l, kg_l, gl_l, ai_l = [], [], [], [], [], []
    for j in range(hb):
        sel = lane == hblk * hb + j
        beta_col = jnp.sum(jnp.where(sel, beta_t, 0.0), axis=-1, keepdims=True).reshape(nc, c, 1)
        gc_col = jnp.sum(jnp.where(sel, gc_t, 0.0), axis=-1, keepdims=True).reshape(nc, c, 1)
        gc_row = gct_ref[0, j]
        decay = jnp.where(incl, jnp.exp(jnp.where(incl, gc_col - gc_row, 0.0)), 0.0)
        eg = jnp.exp(gc_col)
        g_last = gc_col[:, c - 1:c, :]
        kp = k3[j // rep]
        vj = v2[:, j * GDN_DV:(j + 1) * GDN_DV].reshape(nc, c, GDN_DV)
        ai_l.append(_mxu(qk[j // rep] * decay, c))
        m_l.append(jnp.where(strict, kk[j // rep] * (beta_col * decay), 0.0))
        rhs_l.append(_mxu(jnp.concatenate([vj * beta_col, kp * (beta_col * eg)], axis=-1), c))
        qg_l.append(q3[j // rep] * eg)
        kg_l.append(_mxu(kp * jnp.exp(g_last - gc_col), c))
        gl_l.append(jnp.exp(g_last))
    cat = lambda xs: jnp.concatenate(xs, axis=0)
    tinv = _mxu(_unit_lower_inverse(cat(m_l), c), c)
    uw = _bdot(tinv, cat(rhs_l))
    u = uw[:, :, :GDN_DV]
    wq = _mxu(jnp.concatenate([uw[:, :, GDN_DV:], cat(qg_l)], axis=1), c)
    kg, gl, a_intra = cat(kg_l), cat(gl_l), cat(ai_l)

    s = s_scr[...]
    for ci in range(nc):
        idx = [j * nc + ci for j in range(hb)]
        ws_qs = _bdot(_take(wq, idx), _mxu(s, c))
        v_new = _take(u, idx) - ws_qs[:, :c]
        v_newb = _mxu(v_new, c)
        o = ws_qs[:, c:] + _bdot(_take(a_intra, idx), v_newb)
        s = s * _take(gl, idx) + _bdot_tn(_take(kg, idx), v_newb)
        o = o * lax.rsqrt(jnp.mean(o * o, axis=-1, keepdims=True) + RMS_EPS) * nw_ref[...]
        for j in range(hb):
            vslab = slice(j * GDN_DV, (j + 1) * GDN_DV)
            if padded:
                z = z_ref[0, :, vslab]
                o_ref[0, :, vslab] = (o[j][0:l_real] * (z * _sigmoid(z))).astype(o_ref.dtype)
            else:
                z = z_ref[0, ci * c:(ci + 1) * c, vslab]
                o_ref[0, ci * c:(ci + 1) * c, vslab] = (o[j] * (z * _sigmoid(z))).astype(o_ref.dtype)
    s_scr[...] = s

    @pl.when(t == nt - 1)
    def _():
        so_ref[0] = s


def gdn_delta(qkv, proj, beta, gc, gct, norm_w, s0, l_real, chunk, lb, hb):
    b = qkv.shape[0]
    conv_dim = 2 * GDN_HK * GDN_DK + GDN_HV * GDN_DV
    lp = beta.shape[1]
    lrows = l_real if l_real < lb else lb
    nt = lp // lb
    rep = GDN_HV // GDN_HK
    kh = hb // rep
    k_base = GDN_HK // kh
    v_base = (2 * GDN_HK * GDN_DK) // (hb * GDN_DV)
    z_base = conv_dim // (hb * GDN_DV)
    in_specs = [
        pl.BlockSpec((1, lrows, kh * GDN_DK), lambda i, h, t: (i, t, h)),
        pl.BlockSpec((1, lrows, kh * GDN_DK), lambda i, h, t: (i, t, k_base + h)),
        pl.BlockSpec((1, lrows, hb * GDN_DV), lambda i, h, t: (i, t, v_base + h)),
        pl.BlockSpec((1, lrows, hb * GDN_DV), lambda i, h, t: (i, t, z_base + h)),
        pl.BlockSpec((1, lb, GDN_HV), lambda i, h, t: (i, t, 0)),
        pl.BlockSpec((1, lb, GDN_HV), lambda i, h, t: (i, t, 0)),
        pl.BlockSpec((1, hb, lb // chunk, 1, chunk), lambda i, h, t: (i, h, t, 0, 0)),
        pl.BlockSpec((1, GDN_DV), lambda i, h, t: (0, 0)),
    ]
    args = [qkv, qkv, qkv, proj, beta, gc, gct, norm_w.reshape(1, GDN_DV)]
    if s0 is not None:
        in_specs.append(pl.BlockSpec((1, hb, GDN_DK, GDN_DV), lambda i, h, t: (i, h, 0, 0)))
        args.append(s0)
    kern = functools.partial(_gdn_delta_kernel, l_real=l_real, lb=lb, chunk=chunk, hb=hb, has_s0=s0 is not None)
    return pl.pallas_call(
        kern,
        grid=(b, GDN_HV // hb, nt),
        in_specs=in_specs,
        out_specs=[pl.BlockSpec((1, lrows, hb * GDN_DV), lambda i, h, t: (i, t, h)),
                   pl.BlockSpec((1, hb, GDN_DK, GDN_DV), lambda i, h, t: (i, h, 0, 0))],
        out_shape=[jax.ShapeDtypeStruct((b, l_real, GDN_HV * GDN_DV), BF16),
                   jax.ShapeDtypeStruct((b, GDN_HV, GDN_DK, GDN_DV), F32)],
        scratch_shapes=[pltpu.VMEM((hb, GDN_DK, GDN_DV), F32),
                        pltpu.VMEM((lb, kh * GDN_DK), F32),
                        pltpu.VMEM((lb, hb * GDN_DV), F32)],
        compiler_params=_cparams(("parallel", "parallel", "arbitrary")),
        name="gdn_delta",
    )(*args)


def _rwkv_mix_kernel(x_ref, nw_ref, shift_ref, mix_ref, o0, o1, o2, o3, o4, o5, last_ref, win_ref, *, lb):
    t = pl.program_id(1)
    top = SUBLANES

    @pl.when(t == 0)
    def _():
        win_ref[top - 1:top, :] = shift_ref[0]

    x = x_ref[0]
    h = x * lax.rsqrt(jnp.mean(x * x, axis=-1, keepdims=True) + RMS_EPS) * nw_ref[...]
    win_ref[top:top + lb, :] = h
    prev = win_ref[top - 1:top - 1 + lb, :]
    last = win_ref[top + lb - 1:top + lb, :]
    win_ref[top - 1:top, :] = last
    last_ref[0] = last
    xx = prev - h
    for i, o in enumerate((o0, o1, o2, o3, o4, o5)):
        o[0] = (h + xx * mix_ref[i:i + 1, :]).astype(o.dtype)


def rwkv_mix(x, norm_w, shift, mix):
    b, l, d = x.shape
    lb = _pick(l, (128, 64, 32, 16, 8))
    kern = functools.partial(_rwkv_mix_kernel, lb=lb)
    tok = pl.BlockSpec((1, lb, d), lambda i, t: (i, t, 0))
    row = pl.BlockSpec((1, 1, d), lambda i, t: (i, 0, 0))
    outs = pl.pallas_call(
        kern,
        grid=(b, l // lb),
        in_specs=[tok, pl.BlockSpec((1, d), lambda i, t: (0, 0)), row,
                  pl.BlockSpec((6, d), lambda i, t: (0, 0))],
        out_specs=[tok] * 6 + [row],
        out_shape=[jax.ShapeDtypeStruct((b, l, d), BF16)] * 6 + [jax.ShapeDtypeStruct((b, 1, d), F32)],
        scratch_shapes=[pltpu.VMEM((SUBLANES + lb, d), F32)],
        compiler_params=_cparams(("parallel", "arbitrary")),
        name="rwkv_mix",
    )(x, norm_w.reshape(1, d), shift.reshape(b, 1, d), mix)
    return outs[:6], outs[6].reshape(b, d)


def _lora_kernel(x_ref, w1_ref, w2_ref, b_ref, o_ref, *, act):
    hmid = _dot(x_ref[...], w1_ref[...])
    if act == "tanh":
        hmid = jnp.tanh(hmid)
    elif act == "sigmoid":
        hmid = _sigmoid(hmid)
    o_ref[...] = b_ref[...] + _dot(hmid.astype(BF16), w2_ref[...])


def lora(x, w1, w2, bias, act):
    m, d = x.shape
    r = w1.shape[1]
    n = w2.shape[1]
    bm = _pick(m, (256, 128, 64, 32, 16, 8))
    return pl.pallas_call(
        functools.partial(_lora_kernel, act=act),
        grid=(m // bm,),
        in_specs=[pl.BlockSpec((bm, d), lambda i: (i, 0)),
                  pl.BlockSpec((d, r), lambda i: (0, 0)),
                  pl.BlockSpec((r, n), lambda i: (0, 0)),
                  pl.BlockSpec((1, n), lambda i: (0, 0))],
        out_specs=pl.BlockSpec((bm, n), lambda i: (i, 0)),
        out_shape=jax.ShapeDtypeStruct((m, n), F32),
        compiler_params=_cparams(("parallel",)),
        name="rwkv_lora",
    )(x, w1, w2, bias.reshape(1, n))


def _rwkv_wkv_kernel(*refs, l_real, lb, chunk, has_s0):
    n_in = 12 if has_s0 else 11
    (r_ref, k_ref, v_ref, w_ref, a_ref, g_ref, kk_ref, ka_ref, rk_ref, lnw_ref, lnb_ref) = refs[:11]
    s0_ref = refs[11] if has_s0 else None
    o_ref, so_ref, s_scr, pad_scr = refs[n_in:]
    t = pl.program_id(2)
    nt = pl.num_programs(2)
    n = RWKV_N
    c = chunk
    nc = lb // c
    width = r_ref.shape[-1]
    heads = width // n
    padded = l_real < lb
    incl, strict, _ = _tri_masks(c)

    @pl.when(t == 0)
    def _():
        if has_s0:
            s_scr[...] = s0_ref[0]
        else:
            s_scr[...] = jnp.zeros_like(s_scr)

    def load(ref):
        if padded:
            pad_scr[...] = jnp.zeros_like(pad_scr)
            pad_scr[0:l_real, :] = ref[0]
            return pad_scr[...]
        return ref[0]

    r2, k2, v2, w2, a2, g2 = (load(ref) for ref in (r_ref, k_ref, v_ref, w_ref, a_ref, g_ref))
    lw2 = -jnp.exp(-_softplus(-w2) - 0.5)
    if padded:
        rows = lax.broadcasted_iota(jnp.int32, lw2.shape, 0)
        lw2 = jnp.where(rows < l_real, lw2, 0.0)
    a_sig = _sigmoid(a2)
    kmod = k2 * (1.0 + (a_sig - 1.0) * ka_ref[...])
    cl2 = _chunk_cumsum(lw2.reshape(nc, c, width), c).reshape(lb, width)
    ep2 = jnp.exp(cl2)
    em2 = jnp.exp(-cl2)

    def split(x2):
        x3 = x2.reshape(nc, c, width)
        return jnp.concatenate([x3[:, :, h * n:(h + 1) * n] for h in range(heads)], axis=0)

    def per_head(p_ref):
        return jnp.concatenate([jnp.broadcast_to(p_ref[:, h * n:(h + 1) * n][None], (nc, 1, n))
                                for h in range(heads)], axis=0)

    kk = split(k2 * kk_ref[...])
    kk = kk * lax.rsqrt(jnp.sum(kk * kk, axis=-1, keepdims=True) + L2_EPS)
    em = split(em2)
    ep = split(ep2)
    a_h = -kk * split(jnp.exp(cl2 - lw2))
    b_h = kk * split(a_sig) * em
    r = split(r2)
    k = split(kmod)
    v = split(v2)
    r_h = r * ep
    k_h = k * em
    pc = ep[:, c - 1:c, :]
    a_hb, b_hb, r_hb, k_hb, vb = (_mxu(x, c) for x in (a_h, b_h, r_h, k_h, v))
    a_ak = _mxu(jnp.where(strict, _bdot_nt(a_hb, k_hb), 0.0), c)
    a_ab = jnp.where(strict, _bdot_nt(a_hb, b_hb), 0.0)
    a_rk = _mxu(jnp.where(incl, _bdot_nt(r_hb, k_hb), 0.0), c)
    a_rb = _mxu(jnp.where(incl, _bdot_nt(r_hb, b_hb), 0.0), c)
    tinv = _mxu(_unit_lower_inverse(-a_ab, c), c)
    wmat = _bdot(tinv, a_hb)
    ut = _bdot(tinv, _mxu(_bdot(a_ak, vb), c))
    rt = r_h + _bdot(a_rb, _mxu(wmat, c))
    yt = _bdot(a_rb, _mxu(ut, c)) + _bdot(a_rk, vb)
    bt = b_h * pc
    btb = _mxu(bt, c)
    hmat = _bdot_tn(_mxu(jnp.concatenate([ut, v], axis=1), c),
                    _mxu(jnp.concatenate([bt, k_h * pc], axis=1), c))
    rw = _mxu(jnp.concatenate([rt, wmat], axis=1), c)

    s = s_scr[...]
    ys = [None] * (heads * nc)
    for ci in range(nc):
        idx = [h * nc + ci for h in range(heads)]
        ys_sw = _bdot_nt(_take(rw, idx), _mxu(s, c))
        y_c = ys_sw[:, :c] + _take(yt, idx)
        swt = _mxu(ys_sw[:, c:], c)
        s = s * _take(pc, idx) + _bdot_tn(swt, _take(btb, idx)) + _take(hmat, idx)
        for j, i in enumerate(idx):
            ys[i] = y_c[j]
    s_scr[...] = s
    y = jnp.stack(ys)

    mu = jnp.mean(y, axis=-1, keepdims=True)
    yc = y - mu
    var = jnp.mean(yc * yc, axis=-1, keepdims=True)
    yn = yc * lax.rsqrt(var + RWKV_LNX_EPS) * per_head(lnw_ref) + per_head(lnb_ref)
    yn = yn + jnp.sum(r * k * per_head(rk_ref), axis=-1, keepdims=True) * v
    out = (yn * split(g2)).astype(o_ref.dtype)
    for h in range(heads):
        rows = out[h * nc:(h + 1) * nc].reshape(lb, n)
        if padded:
            o_ref[0, :, h * n:(h + 1) * n] = rows[0:l_real]
        else:
            o_ref[0, :, h * n:(h + 1) * n] = rows

    @pl.when(t == nt - 1)
    def _():
        so_ref[0] = s


def rwkv_wkv(r, k, v, w, a, g, k_k, k_a, r_k, lnx_w, lnx_b, s0, chunk, lb, heads):
    b, l, d = r.shape
    n = RWKV_N
    h = d // n
    width = heads * n
    lrows = l if l < lb else lb
    nt = max(1, l // lb)
    tok = pl.BlockSpec((1, lrows, width), lambda i, p, t: (i, t, p))
    par = pl.BlockSpec((1, width), lambda i, p, t: (0, p))
    st = pl.BlockSpec((1, heads, n, n), lambda i, p, t: (i, p, 0, 0))
    in_specs = [tok] * 6 + [par] * 5
    args = [r, k, v, w, a, g, k_k.reshape(1, d), k_a.reshape(1, d), r_k.reshape(1, d),
            lnx_w.reshape(1, d), lnx_b.reshape(1, d)]
    if s0 is not None:
        in_specs.append(st)
        args.append(s0)
    kern = functools.partial(_rwkv_wkv_kernel, l_real=l, lb=lb, chunk=chunk, has_s0=s0 is not None)
    return pl.pallas_call(
        kern,
        grid=(b, h // heads, nt),
        in_specs=in_specs,
        out_specs=[tok, st],
        out_shape=[jax.ShapeDtypeStruct((b, l, d), BF16),
                   jax.ShapeDtypeStruct((b, h, n, n), F32)],
        scratch_shapes=[pltpu.VMEM((heads, n, n), F32), pltpu.VMEM((lb, width), F32)],
        compiler_params=_cparams(("parallel", "parallel", "arbitrary")),
        name="rwkv_wkv",
    )(*args)


def _round_up(x, m):
    return -(-x // m) * m


def _gdn_layer(x, conv_st, ssm_st, w):
    b, l, d = x.shape
    conv_dim = 2 * GDN_HK * GDN_DK + GDN_HV * GDN_DV
    x2 = x.reshape(b * l, d)
    h = rmsnorm(x2, w["norm_mix"], BF16)
    main_cols = conv_dim + GDN_HV * GDN_DV
    ba_cols = _round_up(2 * GDN_HV, LANES)
    proj = matmul_ws(h, w["w_in"], 0, n_out=main_cols).reshape(b, l, -1)
    ba = matmul_ws(h, w["w_in"], 0, n_out=ba_cols, col0=main_cols, bn=LANES).reshape(b, l, -1)
    if conv_st is None:
        conv_st = jnp.zeros((b, GDN_CONV_W - 1, conv_dim), F32)
    new_conv = proj[:, l - (GDN_CONV_W - 1):, :conv_dim]
    qkv = gdn_prep(proj, conv_st, w["conv_w"])
    chunk = GDN_CHUNK if l >= GDN_CHUNK else _round_up(l, SUBLANES)
    lp = _round_up(l, chunk)
    beta, gc = gdn_gates(ba, w["a_log"], w["dt_bias"], chunk, lp)
    gct = jnp.swapaxes(gc, 1, 2).reshape(b, GDN_HV, lp // chunk, 1, chunk)
    if l >= GDN_CHUNK:
        lb, hb = _pick(lp, (512, 256, 128, 64)), _pick(GDN_HV, (4, 2))
    else:
        lb, hb = lp, _pick(GDN_HV, (16, 8, 4, 2))
    o, s_new = gdn_delta(qkv, proj, beta, gc, gct, w["norm_w"], ssm_st, l, chunk, lb, hb)
    x2 = matmul(o.reshape(b * l, -1), w["w_out"], 0, res=x2, bk=_pick(o.shape[-1], (4096,)))
    return x2.reshape(b, l, d), new_conv, s_new


def _rwkv_layer(x, shift_st, wkv_st, w):
    b, l, d = x.shape
    if shift_st is None:
        shift_st = jnp.zeros((b, d), F32)
    (xr, xw, xk, xv, xa, xg), new_shift = rwkv_mix(x, w["norm_mix"], shift_st, w["mix"])
    m = b * l
    flat = lambda t: t.reshape(m, d)
    r = matmul_ws(flat(xr), w["w_r"], 0)
    k = matmul_ws(flat(xk), w["w_k"], 0)
    v = matmul_ws(flat(xv), w["w_v"], 0)
    wl = lora(flat(xw), w["w1"], w["w2"], w["w0"], "tanh")
    al = lora(flat(xa), w["a1"], w["a2"], w["a0"], "none")
    g = lora(flat(xg), w["g1"], w["g2"], jnp.zeros((d,), F32), "sigmoid")
    chunk = RWKV_CHUNK if l >= RWKV_CHUNK else _round_up(l, SUBLANES)
    n_heads = d // RWKV_N
    if l >= RWKV_CHUNK:
        lb, heads = _pick(l, (512, 256, 128, 64)), _pick(n_heads, (4, 2))
    else:
        lb, heads = chunk, _pick(n_heads, (32, 16, 8, 4, 2))
    to3 = lambda t: t.reshape(b, l, d)
    y, s_new = rwkv_wkv(to3(r), to3(k), to3(v), to3(wl), to3(al), to3(g), w["k_k"], w["k_a"], w["r_k"],
                        w["lnx_w"], w["lnx_b"], wkv_st, chunk, lb, heads)
    x2 = matmul_ws(y.reshape(m, d), w["w_out"], 0, res=x.reshape(m, d))
    return x2.reshape(b, l, d), new_shift, s_new


def _ffn(x, w, layer):
    b, l, d = x.shape
    x2 = x.reshape(b * l, d)
    h = rmsnorm(x2, w["norm"][layer], BF16)
    act = gate_up(h, w["w_gate"], w["w_up"], layer)
    dff = act.shape[1]
    bk = dff // 2 if (dff // 2) % LANES == 0 else dff
    x2 = matmul(act, w["w_down"], layer, res=x2, bn=_pick(d, (512, 256, 128)), bk=bk)
    return x2.reshape(b, l, d)


def _trunk(x, conv_st, ssm_st, shift_st, wkv_st, wts):
    x, conv_new, ssm_new = _gdn_layer(x, conv_st, ssm_st, wts["gdn"])
    x = _ffn(x, wts["ffn"], 0)
    x, shift_new, wkv_new = _rwkv_layer(x, shift_st, wkv_st, wts["rwkv"])
    x = _ffn(x, wts["ffn"], 1)
    b, l, d = x.shape
    y = rmsnorm(x.reshape(b * l, d), wts["norm_final"], F32).reshape(b, l, d)
    return y, conv_new[None], ssm_new[None], shift_new[None], wkv_new[None]


def _pad_to(x, axis, size):
    pad = size - x.shape[axis]
    if pad == 0:
        return x
    widths = [(0, 0)] * x.ndim
    widths[axis] = (0, pad)
    return jnp.pad(x, widths)


def kernel(x_prompt, x_sample, state_gdn_conv, state_gdn_ssm, state_rwkv_shift, state_rwkv_wkv,
           norm_mix, norm_ffn, norm_final,
           gdn_w_in, gdn_conv_w, gdn_a_log, gdn_dt_bias, gdn_norm_w, gdn_w_out,
           rwkv_mix, rwkv_w_r, rwkv_w_k, rwkv_w_v, rwkv_w0, rwkv_w1, rwkv_w2,
           rwkv_a0, rwkv_a1, rwkv_a2, rwkv_g1, rwkv_g2, rwkv_k_k, rwkv_k_a, rwkv_r_k,
           rwkv_lnx_w, rwkv_lnx_b, rwkv_w_out,
           ffn_w_gate, ffn_w_up, ffn_w_down):
    bf = lambda t: t.astype(BF16)
    gate_rank = _round_up(rwkv_g1.shape[-1], LANES)
    wts = {
        "gdn": {
            "norm_mix": norm_mix[0],
            "w_in": _pad_to(gdn_w_in, 2, _round_up(gdn_w_in.shape[2], LANES)),
            "conv_w": gdn_conv_w[0], "a_log": gdn_a_log[0], "dt_bias": gdn_dt_bias[0],
            "norm_w": gdn_norm_w[0], "w_out": bf(gdn_w_out),
        },
        "rwkv": {
            "norm_mix": norm_mix[1], "mix": rwkv_mix[0],
            "w_r": rwkv_w_r, "w_k": rwkv_w_k, "w_v": rwkv_w_v,
            "w0": rwkv_w0[0], "w1": bf(rwkv_w1[0]), "w2": bf(rwkv_w2[0]),
            "a0": rwkv_a0[0], "a1": bf(rwkv_a1[0]), "a2": bf(rwkv_a2[0]),
            "g1": bf(_pad_to(rwkv_g1[0], 1, gate_rank)), "g2": bf(_pad_to(rwkv_g2[0], 0, gate_rank)),
            "k_k": rwkv_k_k[0], "k_a": rwkv_k_a[0], "r_k": rwkv_r_k[0],
            "lnx_w": rwkv_lnx_w[0], "lnx_b": rwkv_lnx_b[0], "w_out": rwkv_w_out,
        },
        "ffn": {"norm": norm_ffn, "w_gate": ffn_w_gate, "w_up": ffn_w_up, "w_down": bf(ffn_w_down)},
        "norm_final": norm_final,
    }
    outs_p = _trunk(x_prompt, None, None, None, None, wts)
    outs_s = _trunk(x_sample, state_gdn_conv[0], state_gdn_ssm[0], state_rwkv_shift[0], state_rwkv_wkv[0], wts)
    return (outs_p[0], outs_s[0]) + tuple(outs_p[1:]) + tuple(outs_s[1:])
```

```python
import functools

import jax
import jax.numpy as jnp
from jax import lax
from jax.experimental import pallas as pl
from jax.experimental.pallas import tpu as pltpu

F32 = jnp.float32
BF16 = jnp.bfloat16
HIGHEST = lax.Precision.HIGHEST

D_MODEL = 4096
DEPTH = 2
GDN_HK = 32
GDN_HV = 64
GDN_DK = 128
GDN_DV = 128
GDN_CONV_W = 4
GDN_CHUNK = 64
RWKV_N = 64
RWKV_CHUNK = 64
RWKV_LNX_EPS = 64e-5
RMS_EPS = 1e-6
L2_EPS = 1e-6

LANES = 128
SUBLANES = 8
VMEM_LIMIT = 56 * 1024 * 1024


def _cparams(sem):
    return pltpu.CompilerParams(dimension_semantics=sem, vmem_limit_bytes=VMEM_LIMIT)


def _pick(n, prefs):
    for p in prefs:
        if n % p == 0:
            return p
    return n


def _dot(a, b, precision=None):
    return jnp.dot(a, b, preferred_element_type=F32, precision=precision)


def _sigmoid(x):
    return 1.0 / (1.0 + jnp.exp(-x))


def _softplus(x):
    return jnp.maximum(x, 0.0) + jnp.log1p(jnp.exp(-jnp.abs(x)))


def _tri_masks(c):
    ii = lax.broadcasted_iota(jnp.int32, (c, c), 0)
    jj = lax.broadcasted_iota(jnp.int32, (c, c), 1)
    return ii >= jj, ii > jj, ii == jj


def _bdot(a, b):
    return lax.dot_general(a, b, (((2,), (1,)), ((0,), (0,))), preferred_element_type=F32)


def _bdot_nt(a, b):
    return lax.dot_general(a, b, (((2,), (2,)), ((0,), (0,))), preferred_element_type=F32)


def _bdot_tn(a, b):
    return lax.dot_general(a, b, (((1,), (1,)), ((0,), (0,))), preferred_element_type=F32)


def _mxu(x, rows):
    return x.astype(BF16) if rows >= 2 * SUBLANES else x


def _chunk_cumsum(x, c):
    incl, _, _ = _tri_masks(c)
    tri = jnp.broadcast_to(jnp.where(incl, 1.0, 0.0).astype(BF16)[None], (x.shape[0], c, c))
    hi = x.astype(BF16)
    rest = x - hi.astype(F32)
    mid = rest.astype(BF16)
    lo = (rest - mid.astype(F32)).astype(BF16)
    return _bdot(tri, hi) + _bdot(tri, mid) + _bdot(tri, lo)


def _pair_masks(c):
    left = lax.broadcasted_iota(jnp.int32, (1, 1, 2 * c), 2) < c
    ii = lax.broadcasted_iota(jnp.int32, (c, 2 * c), 0)
    jj = lax.broadcasted_iota(jnp.int32, (c, 2 * c), 1)
    jm = jnp.where(jj < c, jj, jj - c)
    return left, ii >= jm, ii > jm, ii == jm


def _bd(x, left):
    return jnp.concatenate([jnp.where(left, x, 0.0), jnp.where(left, 0.0, x)], axis=1)


def _pair_inverse(a, c):
    left, _, _, diag = _pair_masks(c)
    x = jnp.where(diag, 1.0, 0.0).astype(F32)[None] + a
    pw = _bdot(_mxu(a, c), _mxu(_bd(a, left), c))
    power = 2
    while power < c:
        if 2 * power < c:
            y = _bdot(_mxu(pw, c), _mxu(jnp.concatenate([_bd(x, left), _bd(pw, left)], axis=2), c))
            x = x + y[:, :, :2 * c]
            pw = y[:, :, 2 * c:]
        else:
            x = x + _bdot(_mxu(pw, c), _mxu(_bd(x, left), c))
        power *= 2
    return x


def _take(x, idx):
    return jnp.stack([x[i] for i in idx])


def _mm_kernel(*refs, nk, has_res):
    if has_res:
        a_ref, b_ref, r_ref, o_ref = refs[:4]
        scratch = refs[4:]
    else:
        a_ref, b_ref, o_ref = refs[:3]
        r_ref = None
        scratch = refs[3:]
    part = _dot(a_ref[...], b_ref[...])

    def finish(total):
        if has_res:
            total = total + r_ref[...]
        o_ref[...] = total.astype(o_ref.dtype)

    if nk == 1:
        finish(part)
        return
    acc_ref = scratch[0]
    k = pl.program_id(2)

    @pl.when(k == 0)
    def _():
        acc_ref[...] = part

    @pl.when(jnp.logical_and(k > 0, k < nk - 1))
    def _():
        acc_ref[...] += part

    @pl.when(k == nk - 1)
    def _():
        finish(acc_ref[...] + part)


def matmul(a, b, layer, res=None, out_dtype=F32, bm=None, bn=None, bk=None):
    m, kdim = a.shape
    n = b.shape[2]
    bm = bm or _pick(m, (1024, 512, 256, 128, 64, 32, 16, 8))
    bk = bk or kdim
    bn = bn or _pick(n, (1024, 512, 256, 128) if (res is None and bk == kdim) else (512, 256, 128))
    nk = kdim // bk
    in_specs = [pl.BlockSpec((bm, bk), lambda i, j, k: (i, k)),
                pl.BlockSpec((None, bk, bn), lambda i, j, k: (layer, k, j))]
    args = [a, b]
    if res is not None:
        in_specs.append(pl.BlockSpec((bm, bn), lambda i, j, k: (i, j)))
        args.append(res)
    scratch = [pltpu.VMEM((bm, bn), F32)] if nk > 1 else []
    return pl.pallas_call(
        functools.partial(_mm_kernel, nk=nk, has_res=res is not None),
        grid=(m // bm, n // bn, nk),
        in_specs=in_specs,
        out_specs=pl.BlockSpec((bm, bn), lambda i, j, k: (i, j)),
        out_shape=jax.ShapeDtypeStruct((m, n), out_dtype),
        scratch_shapes=scratch,
        compiler_params=_cparams(("parallel", "parallel", "arbitrary")),
        name="matmul",
    )(*args)


def _mm_ws_kernel(*refs, has_res):
    if has_res:
        a_ref, w_ref, r_ref, o_ref, wb_ref = refs
    else:
        a_ref, w_ref, o_ref, wb_ref = refs
        r_ref = None

    @pl.when(pl.program_id(1) == 0)
    def _():
        wb_ref[...] = w_ref[...].astype(BF16)

    total = _dot(a_ref[...], wb_ref[...])
    if has_res:
        total = total + r_ref[...]
    o_ref[...] = total.astype(o_ref.dtype)


def matmul_ws(a, w, layer, n_out=None, col0=0, res=None, out_dtype=F32, bn=None):
    m, kdim = a.shape
    n_out = n_out or w.shape[2]
    bm = _pick(m, (1024, 512, 256, 128, 64, 32, 16, 8))
    bn = bn or _pick(n_out, (512, 256, 128))
    assert col0 % bn == 0 and n_out % bn == 0
    off = col0 // bn
    in_specs = [pl.BlockSpec((bm, kdim), lambda j, i: (i, 0)),
                pl.BlockSpec((None, kdim, bn), lambda j, i: (layer, 0, off + j))]
    args = [a, w]
    if res is not None:
        in_specs.append(pl.BlockSpec((bm, bn), lambda j, i: (i, j)))
        args.append(res)
    return pl.pallas_call(
        functools.partial(_mm_ws_kernel, has_res=res is not None),
        grid=(n_out // bn, m // bm),
        in_specs=in_specs,
        out_specs=pl.BlockSpec((bm, bn), lambda j, i: (i, j)),
        out_shape=jax.ShapeDtypeStruct((m, n_out), out_dtype),
        scratch_shapes=[pltpu.VMEM((kdim, bn), BF16)],
        compiler_params=_cparams(("arbitrary", "arbitrary")),
        name="matmul_ws",
    )(*args)


def _gate_up_kernel(h_ref, wg_ref, wu_ref, o_ref, wgb_ref, wub_ref):
    @pl.when(pl.program_id(1) == 0)
    def _():
        wgb_ref[...] = wg_ref[...].astype(BF16)
        wub_ref[...] = wu_ref[...].astype(BF16)

    h = h_ref[...]
    g = _dot(h, wgb_ref[...])
    u = _dot(h, wub_ref[...])
    o_ref[...] = (g * _sigmoid(g) * u).astype(o_ref.dtype)


def gate_up(h, wg, wu, layer):
    m, kdim = h.shape
    n = wg.shape[2]
    bm = _pick(m, (1024, 512, 256, 128, 64, 32, 16, 8))
    bn = _pick(n, (256, 128))
    wspec = pl.BlockSpec((None, kdim, bn), lambda j, i: (layer, 0, j))
    return pl.pallas_call(
        _gate_up_kernel,
        grid=(n // bn, m // bm),
        in_specs=[pl.BlockSpec((bm, kdim), lambda j, i: (i, 0)), wspec, wspec],
        out_specs=pl.BlockSpec((bm, bn), lambda j, i: (i, j)),
        out_shape=jax.ShapeDtypeStruct((m, n), BF16),
        scratch_shapes=[pltpu.VMEM((kdim, bn), BF16), pltpu.VMEM((kdim, bn), BF16)],
        compiler_params=_cparams(("arbitrary", "arbitrary")),
        name="ffn_gate_up",
    )(h, wg, wu)


def _rms_kernel(x_ref, w_ref, o_ref):
    x = x_ref[...]
    y = x * lax.rsqrt(jnp.mean(x * x, axis=-1, keepdims=True) + RMS_EPS)
    o_ref[...] = (y * w_ref[...]).astype(o_ref.dtype)


def rmsnorm(x, w, out_dtype):
    m, d = x.shape
    br = _pick(m, (256, 128, 64, 32, 16, 8))
    return pl.pallas_call(
        _rms_kernel,
        grid=(m // br,),
        in_specs=[pl.BlockSpec((br, d), lambda i: (i, 0)),
                  pl.BlockSpec((1, d), lambda i: (0, 0))],
        out_specs=pl.BlockSpec((br, d), lambda i: (i, 0)),
        out_shape=jax.ShapeDtypeStruct((m, d), out_dtype),
        compiler_params=_cparams(("parallel",)),
        name="rmsnorm",
    )(x, w.reshape(1, d))


def _gdn_prep_kernel(x_ref, buf_ref, cw_ref, o_ref, win_ref, *, lb, n_q_blocks, n_qk_blocks, q_scale):
    cblk = pl.program_id(1)
    l = pl.program_id(2)
    halo = GDN_CONV_W - 1
    top = SUBLANES

    @pl.when(l == 0)
    def _():
        win_ref[:, top - halo:top, :] = buf_ref[...]

    win_ref[:, top:top + lb, :] = x_ref[...]
    acc = win_ref[:, top - halo:top - halo + lb, :] * cw_ref[0:1, :]
    for j in range(1, GDN_CONV_W):
        acc = acc + win_ref[:, top - halo + j:top - halo + j + lb, :] * cw_ref[j:j + 1, :]
    carry = win_ref[:, top + lb - halo:top + lb, :]
    win_ref[:, top - halo:top, :] = carry
    y = acc * _sigmoid(acc)

    @pl.when(cblk >= n_qk_blocks)
    def _():
        o_ref[...] = y

    @pl.when(cblk < n_qk_blocks)
    def _():
        scale = jnp.where(cblk < n_q_blocks, q_scale, 1.0).astype(F32)
        for s in range(y.shape[-1] // GDN_DK):
            ys = y[:, :, s * GDN_DK:(s + 1) * GDN_DK]
            inv = lax.rsqrt(jnp.sum(ys * ys, axis=-1, keepdims=True) + L2_EPS) * scale
            o_ref[:, :, s * GDN_DK:(s + 1) * GDN_DK] = ys * inv


def gdn_prep(proj, conv_buf, conv_w):
    b, l, _ = proj.shape
    kdim = GDN_HK * GDN_DK
    conv_dim = conv_w.shape[1]
    cb = _pick(kdim, (2048, 1024, 512, 256, 128))
    lb = _pick(l, (256, 128, 64, 32, 16, 8))
    bb = _pick(b, (16, 8, 4, 2, 1)) if l < SUBLANES else 1
    kern = functools.partial(_gdn_prep_kernel, lb=lb, n_q_blocks=kdim // cb, n_qk_blocks=2 * kdim // cb,
                             q_scale=float(GDN_DK) ** -0.5)
    return pl.pallas_call(
        kern,
        grid=(b // bb, conv_dim // cb, l // lb),
        in_specs=[pl.BlockSpec((bb, lb, cb), lambda i, c, t: (i, t, c)),
                  pl.BlockSpec((bb, GDN_CONV_W - 1, cb), lambda i, c, t: (i, 0, c)),
                  pl.BlockSpec((GDN_CONV_W, cb), lambda i, c, t: (0, c))],
        out_specs=pl.BlockSpec((bb, lb, cb), lambda i, c, t: (i, t, c)),
        out_shape=jax.ShapeDtypeStruct((b, l, conv_dim), F32),
        scratch_shapes=[pltpu.VMEM((bb, SUBLANES + lb, cb), F32)],
        compiler_params=_cparams(("parallel", "parallel", "arbitrary")),
        name="gdn_prep",
    )(proj, conv_buf, conv_w)


def _gdn_gates_kernel(ba_ref, alog_ref, dtb_ref, beta_ref, gc_ref, pad_ref, *, lb, lpb, chunk):
    hv = GDN_HV
    ba = ba_ref[0]
    beta = _sigmoid(ba[:, :hv])
    g = -jnp.exp(alog_ref[...]) * _softplus(ba[:, hv:2 * hv] + dtb_ref[...])
    if lpb != lb:
        pad_ref[...] = jnp.zeros_like(pad_ref)
        pad_ref[0:lb, :] = g
        g = pad_ref[...]
        pad_ref[0:lb, :] = beta
        beta = pad_ref[...]
    ii = lax.broadcasted_iota(jnp.int32, (lpb, lpb), 0)
    jj = lax.broadcasted_iota(jnp.int32, (lpb, lpb), 1)
    same_chunk_causal = jnp.logical_and(ii >= jj, ii // chunk == jj // chunk)
    gc = _dot(jnp.where(same_chunk_causal, 1.0, 0.0).astype(F32), g, HIGHEST)
    beta_ref[0] = beta
    gc_ref[0] = gc


def gdn_gates(ba, a_log, dt_bias, chunk, lp):
    b, l, w = ba.shape
    lb = _pick(l, (256, 128, 64, 32, 16, 8))
    lpb = lb if lp == l else lp
    kern = functools.partial(_gdn_gates_kernel, lb=lb, lpb=lpb, chunk=chunk)
    out = jax.ShapeDtypeStruct((b, lp, GDN_HV), F32)
    return pl.pallas_call(
        kern,
        grid=(b, l // lb),
        in_specs=[pl.BlockSpec((1, lb, w), lambda i, t: (i, t, 0)),
                  pl.BlockSpec((1, GDN_HV), lambda i, t: (0, 0)),
                  pl.BlockSpec((1, GDN_HV), lambda i, t: (0, 0))],
        out_specs=[pl.BlockSpec((1, lpb, GDN_HV), lambda i, t: (i, t, 0)),
                   pl.BlockSpec((1, lpb, GDN_HV), lambda i, t: (i, t, 0))],
        out_shape=[out, out],
        scratch_shapes=[pltpu.VMEM((lpb, GDN_HV), F32)],
        compiler_params=_cparams(("parallel", "parallel")),
        name="gdn_gates",
    )(ba, a_log.reshape(1, GDN_HV), dt_bias.reshape(1, GDN_HV))


def _gdn_delta_kernel(*refs, l_real, lb, chunk, hb, has_s0):
    if has_s0:
        q_ref, k_ref, v_ref, z_ref, beta_ref, gc_ref, gct_ref, nw_ref, s0_ref = refs[:9]
        rest = refs[9:]
    else:
        q_ref, k_ref, v_ref, z_ref, beta_ref, gc_ref, gct_ref, nw_ref = refs[:8]
        s0_ref = None
        rest = refs[8:]
    o_ref, so_ref, s_scr, padk_scr, padv_scr = rest
    hblk = pl.program_id(1)
    t = pl.program_id(2)
    nt = pl.num_programs(2)
    c = chunk
    nc = lb // c
    padded = l_real < lb
    rep = GDN_HV // GDN_HK
    kh = hb // rep
    pairs = hb // 2
    left_c, incl, strict, _ = _pair_masks(c)

    @pl.when(t == 0)
    def _():
        if has_s0:
            s_scr[...] = s0_ref[0]
        else:
            s_scr[...] = jnp.zeros_like(s_scr)

    def load(ref, pad_scr):
        if padded:
            pad_scr[...] = jnp.zeros_like(pad_scr)
            pad_scr[0:l_real, :] = ref[0]
            return pad_scr[...]
        return ref[0]

    q2 = load(q_ref, padk_scr)
    k2 = load(k_ref, padk_scr)
    v2 = load(v_ref, padv_scr)
    beta_t = beta_ref[0]
    gc_t = gc_ref[0]
    lane = lax.broadcasted_iota(jnp.int32, (lb, GDN_HV), 1)

    q3, k3, qk2, kk2 = [], [], [], []
    for p in range(kh):
        qp = q2[:, p * GDN_DK:(p + 1) * GDN_DK].reshape(nc, c, GDN_DK)
        kp = k2[:, p * GDN_DK:(p + 1) * GDN_DK].reshape(nc, c, GDN_DK)
        q3.append(qp)
        k3.append(kp)
        kp_twice = _mxu(jnp.concatenate([kp, kp], axis=1), c)
        qk2.append(_bdot_nt(_mxu(qp, c), kp_twice))
        kk2.append(_bdot_nt(_mxu(kp, c), kp_twice))

    def column(tile, head):
        return jnp.sum(jnp.where(lane == head, tile, 0.0), axis=-1, keepdims=True).reshape(nc, c, 1)

    zero = jnp.zeros((nc, c, GDN_DV + GDN_DK), F32)
    m_l, rhs_l, ai_l = [], [], []
    wq_l, kg_l, gl_l = [], [], []
    for pp in range(pairs):
        kp, qp = k3[2 * pp // rep], q3[2 * pp // rep]
        beta = [column(beta_t, hblk * hb + 2 * pp + e) for e in range(2)]
        gcol = [column(gc_t, hblk * hb + 2 * pp + e) for e in range(2)]
        grow = gct_ref[0, pp]
        diff = jnp.where(left_c, gcol[0], gcol[1]) - grow
        decay = jnp.where(incl, jnp.exp(jnp.where(incl, diff, 0.0)), 0.0)
        ai_l.append(_mxu(qk2[2 * pp // rep] * decay, c))
        m_l.append(jnp.where(strict, kk2[2 * pp // rep] * (jnp.where(left_c, beta[0], beta[1]) * decay), 0.0))
        blocks = []
        for e in range(2):
            j = 2 * pp + e
            eg = jnp.exp(gcol[e])
            g_last = gcol[e][:, c - 1:c, :]
            vj = v2[:, j * GDN_DV:(j + 1) * GDN_DV].reshape(nc, c, GDN_DV)
            blocks.append(jnp.concatenate([vj * beta[e], kp * (beta[e] * eg)], axis=-1))
            wq_l.append(qp * eg)
            kg_l.append(_mxu(kp * jnp.exp(g_last - gcol[e]), c))
            gl_l.append(jnp.exp(g_last))
        rhs_l.append(jnp.concatenate([jnp.concatenate([blocks[0], zero], axis=-1),
                                      jnp.concatenate([zero, blocks[1]], axis=-1)], axis=1))
    cat = lambda xs: jnp.concatenate(xs, axis=0)
    tinv = _mxu(_pair_inverse(-cat(m_l), c), c)
    uw = _bdot(tinv, _mxu(cat(rhs_l), c))
    a_intra = cat(ai_l)
    width = GDN_DV + GDN_DK
    u_l, wq_rows = [], []
    for pp in range(pairs):
        for e in range(2):
            part = uw[pp * nc:(pp + 1) * nc, :, e * width:(e + 1) * width]
            u_l.append(part[:, :, :GDN_DV])
            wq_rows.append(_mxu(jnp.concatenate([part[:, :, GDN_DV:], wq_l[2 * pp + e]], axis=1), c))

    s = s_scr[...]
    zero_v = jnp.zeros((pairs, c, GDN_DV), F32)
    for ci in range(nc):
        pick = lambda xs: jnp.stack([x[ci] for x in xs])
        ws_qs = _bdot(pick(wq_rows), _mxu(s, c))
        v_new = pick(u_l) - ws_qs[:, :c]
        v_newb = _mxu(v_new, c)
        v_even = jnp.stack([v_new[2 * pp] for pp in range(pairs)])
        v_odd = jnp.stack([v_new[2 * pp + 1] for pp in range(pairs)])
        v_bd = jnp.concatenate([jnp.concatenate([v_even, zero_v], axis=-1),
                                jnp.concatenate([zero_v, v_odd], axis=-1)], axis=1)
        av = _bdot(_take(a_intra, [pp * nc + ci for pp in range(pairs)]), _mxu(v_bd, c))
        o = ws_qs[:, c:] + jnp.stack([av[j // 2][:, (j % 2) * GDN_DV:(j % 2 + 1) * GDN_DV] for j in range(hb)])
        s = s * pick(gl_l) + _bdot_tn(pick(kg_l), v_newb)
        o = o * lax.rsqrt(jnp.mean(o * o, axis=-1, keepdims=True) + RMS_EPS) * nw_ref[...]
        for j in range(hb):
            vslab = slice(j * GDN_DV, (j + 1) * GDN_DV)
            if padded:
                z = z_ref[0, :, vslab]
                o_ref[0, :, vslab] = (o[j][0:l_real] * (z * _sigmoid(z))).astype(o_ref.dtype)
            else:
                z = z_ref[0, ci * c:(ci + 1) * c, vslab]
                o_ref[0, ci * c:(ci + 1) * c, vslab] = (o[j] * (z * _sigmoid(z))).astype(o_ref.dtype)
    s_scr[...] = s

    @pl.when(t == nt - 1)
    def _():
        so_ref[0] = s


def gdn_delta(qkv, proj, beta, gc, gct, norm_w, s0, l_real, chunk, lb, hb):
    b = qkv.shape[0]
    conv_dim = 2 * GDN_HK * GDN_DK + GDN_HV * GDN_DV
    lp = beta.shape[1]
    lrows = l_real if l_real < lb else lb
    nt = lp // lb
    rep = GDN_HV // GDN_HK
    kh = hb // rep
    k_base = GDN_HK // kh
    v_base = (2 * GDN_HK * GDN_DK) // (hb * GDN_DV)
    z_base = conv_dim // (hb * GDN_DV)
    in_specs = [
        pl.BlockSpec((1, lrows, kh * GDN_DK), lambda i, h, t: (i, t, h)),
        pl.BlockSpec((1, lrows, kh * GDN_DK), lambda i, h, t: (i, t, k_base + h)),
        pl.BlockSpec((1, lrows, hb * GDN_DV), lambda i, h, t: (i, t, v_base + h)),
        pl.BlockSpec((1, lrows, hb * GDN_DV), lambda i, h, t: (i, t, z_base + h)),
        pl.BlockSpec((1, lb, GDN_HV), lambda i, h, t: (i, t, 0)),
        pl.BlockSpec((1, lb, GDN_HV), lambda i, h, t: (i, t, 0)),
        pl.BlockSpec((1, hb // 2, lb // chunk, 1, 2 * chunk), lambda i, h, t: (i, h, t, 0, 0)),
        pl.BlockSpec((1, GDN_DV), lambda i, h, t: (0, 0)),
    ]
    args = [qkv, qkv, qkv, proj, beta, gc, gct, norm_w.reshape(1, GDN_DV)]
    if s0 is not None:
        in_specs.append(pl.BlockSpec((1, hb, GDN_DK, GDN_DV), lambda i, h, t: (i, h, 0, 0)))
        args.append(s0)
    kern = functools.partial(_gdn_delta_kernel, l_real=l_real, lb=lb, chunk=chunk, hb=hb, has_s0=s0 is not None)
    return pl.pallas_call(
        kern,
        grid=(b, GDN_HV // hb, nt),
        in_specs=in_specs,
        out_specs=[pl.BlockSpec((1, lrows, hb * GDN_DV), lambda i, h, t: (i, t, h)),
                   pl.BlockSpec((1, hb, GDN_DK, GDN_DV), lambda i, h, t: (i, h, 0, 0))],
        out_shape=[jax.ShapeDtypeStruct((b, l_real, GDN_HV * GDN_DV), BF16),
                   jax.ShapeDtypeStruct((b, GDN_HV, GDN_DK, GDN_DV), F32)],
        scratch_shapes=[pltpu.VMEM((hb, GDN_DK, GDN_DV), F32),
                        pltpu.VMEM((lb, kh * GDN_DK), F32),
                        pltpu.VMEM((lb, hb * GDN_DV), F32)],
        compiler_params=_cparams(("parallel", "parallel", "arbitrary")),
        name="gdn_delta",
    )(*args)


def _rwkv_mix_kernel(x_ref, nw_ref, shift_ref, mix_ref, o0, o1, o2, o3, o4, o5, last_ref, win_ref, *, lb):
    t = pl.program_id(1)
    top = SUBLANES

    @pl.when(t == 0)
    def _():
        win_ref[top - 1:top, :] = shift_ref[0]

    x = x_ref[0]
    h = x * lax.rsqrt(jnp.mean(x * x, axis=-1, keepdims=True) + RMS_EPS) * nw_ref[...]
    win_ref[top:top + lb, :] = h
    prev = win_ref[top - 1:top - 1 + lb, :]
    last = win_ref[top + lb - 1:top + lb, :]
    win_ref[top - 1:top, :] = last
    last_ref[0] = last
    xx = prev - h
    for i, o in enumerate((o0, o1, o2, o3, o4, o5)):
        o[0] = (h + xx * mix_ref[i:i + 1, :]).astype(o.dtype)


def rwkv_mix(x, norm_w, shift, mix):
    b, l, d = x.shape
    lb = _pick(l, (128, 64, 32, 16, 8))
    kern = functools.partial(_rwkv_mix_kernel, lb=lb)
    tok = pl.BlockSpec((1, lb, d), lambda i, t: (i, t, 0))
    row = pl.BlockSpec((1, 1, d), lambda i, t: (i, 0, 0))
    outs = pl.pallas_call(
        kern,
        grid=(b, l // lb),
        in_specs=[tok, pl.BlockSpec((1, d), lambda i, t: (0, 0)), row,
                  pl.BlockSpec((6, d), lambda i, t: (0, 0))],
        out_specs=[tok] * 6 + [row],
        out_shape=[jax.ShapeDtypeStruct((b, l, d), BF16)] * 6 + [jax.ShapeDtypeStruct((b, 1, d), F32)],
        scratch_shapes=[pltpu.VMEM((SUBLANES + lb, d), F32)],
        compiler_params=_cparams(("parallel", "arbitrary")),
        name="rwkv_mix",
    )(x, norm_w.reshape(1, d), shift.reshape(b, 1, d), mix)
    return outs[:6], outs[6].reshape(b, d)


def _lora_kernel(x_ref, w1_ref, w2_ref, b_ref, o_ref, *, act):
    hmid = _dot(x_ref[...], w1_ref[...])
    if act == "tanh":
        hmid = jnp.tanh(hmid)
    elif act == "sigmoid":
        hmid = _sigmoid(hmid)
    o_ref[...] = b_ref[...] + _dot(hmid.astype(BF16), w2_ref[...])


def lora(x, w1, w2, bias, act):
    m, d = x.shape
    r = w1.shape[1]
    n = w2.shape[1]
    bm = _pick(m, (256, 128, 64, 32, 16, 8))
    return pl.pallas_call(
        functools.partial(_lora_kernel, act=act),
        grid=(m // bm,),
        in_specs=[pl.BlockSpec((bm, d), lambda i: (i, 0)),
                  pl.BlockSpec((d, r), lambda i: (0, 0)),
                  pl.BlockSpec((r, n), lambda i: (0, 0)),
                  pl.BlockSpec((1, n), lambda i: (0, 0))],
        out_specs=pl.BlockSpec((bm, n), lambda i: (i, 0)),
        out_shape=jax.ShapeDtypeStruct((m, n), F32),
        compiler_params=_cparams(("parallel",)),
        name="rwkv_lora",
    )(x, w1, w2, bias.reshape(1, n))


def _rwkv_pair_kernel(*refs, l_real, lb, chunk, has_s0):
    n_in = 12 if has_s0 else 11
    (r_ref, k_ref, v_ref, w_ref, a_ref, g_ref, kk_ref, ka_ref, rk_ref, lnw_ref, lnb_ref) = refs[:11]
    s0_ref = refs[11] if has_s0 else None
    o_ref, so_ref, s_scr, pad_scr = refs[n_in:]
    t = pl.program_id(2)
    nt = pl.num_programs(2)
    n = RWKV_N
    c = chunk
    nc = lb // c
    width = r_ref.shape[-1]
    pairs = width // (2 * n)
    padded = l_real < lb
    left_n = lax.broadcasted_iota(jnp.int32, (1, 1, 2 * n), 2) < n
    _, incl, strict, _ = _pair_masks(c)
    rr = lax.broadcasted_iota(jnp.int32, (2 * n, 2 * n), 0) < n
    cc = lax.broadcasted_iota(jnp.int32, (2 * n, 2 * n), 1) < n
    same_head = rr == cc
    bd = _bd

    def head_sum(x):
        sl = jnp.sum(jnp.where(left_n, x, 0.0), axis=-1, keepdims=True)
        sr = jnp.sum(jnp.where(left_n, 0.0, x), axis=-1, keepdims=True)
        return jnp.where(left_n, sl, sr)

    @pl.when(t == 0)
    def _():
        for p in range(pairs):
            if has_s0:
                zero = jnp.zeros((n, n), F32)
                top = jnp.concatenate([s0_ref[0, 2 * p], zero], axis=1)
                bot = jnp.concatenate([zero, s0_ref[0, 2 * p + 1]], axis=1)
                s_scr[p] = jnp.concatenate([top, bot], axis=0)
            else:
                s_scr[p] = jnp.zeros((2 * n, 2 * n), F32)

    def load(ref):
        if padded:
            pad_scr[...] = jnp.zeros_like(pad_scr)
            pad_scr[0:l_real, :] = ref[0]
            return pad_scr[...]
        return ref[0]

    r2, k2, v2, w2, a2, g2 = (load(ref) for ref in (r_ref, k_ref, v_ref, w_ref, a_ref, g_ref))
    lw2 = -jnp.exp(-_softplus(-w2) - 0.5)
    if padded:
        rows = lax.broadcasted_iota(jnp.int32, lw2.shape, 0)
        lw2 = jnp.where(rows < l_real, lw2, 0.0)
    a_sig = _sigmoid(a2)
    kmod = k2 * (1.0 + (a_sig - 1.0) * ka_ref[...])
    cl2 = _chunk_cumsum(lw2.reshape(nc, c, width), c).reshape(lb, width)

    def split(x2):
        x3 = x2.reshape(nc, c, width)
        return jnp.concatenate([x3[:, :, p * 2 * n:(p + 1) * 2 * n] for p in range(pairs)], axis=0)

    def per_pair(p_ref):
        return jnp.concatenate([jnp.broadcast_to(p_ref[:, p * 2 * n:(p + 1) * 2 * n][None], (nc, 1, 2 * n))
                                for p in range(pairs)], axis=0)

    kk = split(k2 * kk_ref[...])
    kk = kk * lax.rsqrt(head_sum(kk * kk) + L2_EPS)
    ep = split(jnp.exp(cl2))
    em = split(jnp.exp(-cl2))
    a_h = -kk * split(jnp.exp(cl2 - lw2))
    b_h = kk * split(a_sig) * em
    r = split(r2)
    k = split(kmod)
    v = split(v2)
    r_h = r * ep
    k_h = k * em
    pc = ep[:, c - 1:c, :]

    amat = _bdot_nt(_mxu(jnp.concatenate([a_h, r_h], axis=1), c),
                    _mxu(jnp.concatenate([bd(k_h, left_n), bd(b_h, left_n)], axis=1), c))
    a_ak = jnp.where(strict, amat[:, :c, :2 * c], 0.0)
    a_ab = jnp.where(strict, amat[:, :c, 2 * c:], 0.0)
    a_rk = jnp.where(incl, amat[:, c:, :2 * c], 0.0)
    a_rb = jnp.where(incl, amat[:, c:, 2 * c:], 0.0)

    tinv = _mxu(_pair_inverse(a_ab, c), c)

    vbd = bd(v, left_n)
    akv = _bdot(_mxu(a_ak, c), _mxu(vbd, c))
    wu = _bdot(tinv, _mxu(jnp.concatenate([bd(a_h, left_n), bd(akv, left_n)], axis=2), c))
    wmat = wu[:, :, :2 * n]
    ut = wu[:, :, 2 * n:]
    top = jnp.concatenate([bd(wmat, left_n), bd(ut, left_n)], axis=2)
    bot = jnp.concatenate([jnp.zeros_like(vbd), vbd], axis=2)
    ry = _bdot(_mxu(jnp.concatenate([a_rb, a_rk], axis=2), c), _mxu(jnp.concatenate([top, bot], axis=1), c))
    rt = r_h + ry[:, :, :2 * n]
    yt = ry[:, :, 2 * n:]
    bt = b_h * pc
    btb = _mxu(bt, c)
    hmat = _bdot_tn(_mxu(jnp.concatenate([ut, v], axis=1), c),
                    _mxu(jnp.concatenate([bt, k_h * pc], axis=1), c))
    hmat = jnp.where(same_head, hmat, 0.0)
    rw = _mxu(jnp.concatenate([rt, wmat], axis=1), c)

    s = s_scr[...]
    ys = [None] * (pairs * nc)
    for ci in range(nc):
        idx = [p * nc + ci for p in range(pairs)]
        ys_sw = _bdot_nt(_take(rw, idx), _mxu(s, c))
        y_c = ys_sw[:, :c] + _take(yt, idx)
        swt = _mxu(ys_sw[:, c:], c)
        s = s * _take(pc, idx) + jnp.where(same_head, _bdot_tn(swt, _take(btb, idx)), 0.0) + _take(hmat, idx)
        for j, i in enumerate(idx):
            ys[i] = y_c[j]
    s_scr[...] = s
    y = jnp.stack(ys)

    yc = y - head_sum(y) * (1.0 / n)
    var = head_sum(yc * yc) * (1.0 / n)
    yn = yc * lax.rsqrt(var + RWKV_LNX_EPS) * per_pair(lnw_ref) + per_pair(lnb_ref)
    yn = yn + head_sum(r * k * per_pair(rk_ref)) * v
    out = (yn * split(g2)).astype(o_ref.dtype)
    for p in range(pairs):
        rows = out[p * nc:(p + 1) * nc].reshape(lb, 2 * n)
        if padded:
            o_ref[0, :, p * 2 * n:(p + 1) * 2 * n] = rows[0:l_real]
        else:
            o_ref[0, :, p * 2 * n:(p + 1) * 2 * n] = rows

    @pl.when(t == nt - 1)
    def _():
        for p in range(pairs):
            so_ref[0, 2 * p] = s[p][:n, :n]
            so_ref[0, 2 * p + 1] = s[p][n:, n:]


def rwkv_wkv(r, k, v, w, a, g, k_k, k_a, r_k, lnx_w, lnx_b, s0, chunk, lb, heads):
    b, l, d = r.shape
    n = RWKV_N
    h = d // n
    width = heads * n
    lrows = l if l < lb else lb
    nt = max(1, l // lb)
    tok = pl.BlockSpec((1, lrows, width), lambda i, p, t: (i, t, p))
    par = pl.BlockSpec((1, width), lambda i, p, t: (0, p))
    st = pl.BlockSpec((1, heads, n, n), lambda i, p, t: (i, p, 0, 0))
    in_specs = [tok] * 6 + [par] * 5
    args = [r, k, v, w, a, g, k_k.reshape(1, d), k_a.reshape(1, d), r_k.reshape(1, d),
            lnx_w.reshape(1, d), lnx_b.reshape(1, d)]
    if s0 is not None:
        in_specs.append(st)
        args.append(s0)
    kern = functools.partial(_rwkv_pair_kernel, l_real=l, lb=lb, chunk=chunk, has_s0=s0 is not None)
    return pl.pallas_call(
        kern,
        grid=(b, h // heads, nt),
        in_specs=in_specs,
        out_specs=[tok, st],
        out_shape=[jax.ShapeDtypeStruct((b, l, d), BF16),
                   jax.ShapeDtypeStruct((b, h, n, n), F32)],
        scratch_shapes=[pltpu.VMEM((heads // 2, 2 * n, 2 * n), F32), pltpu.VMEM((lb, width), F32)],
        compiler_params=_cparams(("parallel", "parallel", "arbitrary")),
        name="rwkv_wkv",
    )(*args)


def _round_up(x, m):
    return -(-x // m) * m


def _gdn_layer(x, conv_st, ssm_st, w):
    b, l, d = x.shape
    conv_dim = 2 * GDN_HK * GDN_DK + GDN_HV * GDN_DV
    x2 = x.reshape(b * l, d)
    h = rmsnorm(x2, w["norm_mix"], BF16)
    main_cols = conv_dim + GDN_HV * GDN_DV
    ba_cols = _round_up(2 * GDN_HV, LANES)
    proj = matmul_ws(h, w["w_in"], 0, n_out=main_cols).reshape(b, l, -1)
    ba = matmul_ws(h, w["w_in"], 0, n_out=ba_cols, col0=main_cols, bn=LANES).reshape(b, l, -1)
    if conv_st is None:
        conv_st = jnp.zeros((b, GDN_CONV_W - 1, conv_dim), F32)
    new_conv = proj[:, l - (GDN_CONV_W - 1):, :conv_dim]
    qkv = gdn_prep(proj, conv_st, w["conv_w"])
    chunk = GDN_CHUNK if l >= GDN_CHUNK else _round_up(l, SUBLANES)
    lp = _round_up(l, chunk)
    beta, gc = gdn_gates(ba, w["a_log"], w["dt_bias"], chunk, lp)
    gct = jnp.swapaxes(gc, 1, 2).reshape(b, GDN_HV // 2, 2, lp // chunk, chunk)
    gct = jnp.transpose(gct, (0, 1, 3, 2, 4)).reshape(b, GDN_HV // 2, lp // chunk, 1, 2 * chunk)
    if l >= GDN_CHUNK:
        lb, hb = _pick(lp, (512, 256, 128, 64)), _pick(GDN_HV, (8, 4, 2))
    else:
        lb, hb = lp, _pick(GDN_HV, (16, 8, 4, 2))
    o, s_new = gdn_delta(qkv, proj, beta, gc, gct, w["norm_w"], ssm_st, l, chunk, lb, hb)
    x2 = matmul(o.reshape(b * l, -1), w["w_out"], 0, res=x2, bk=_pick(o.shape[-1], (4096,)))
    return x2.reshape(b, l, d), new_conv, s_new


def _rwkv_layer(x, shift_st, wkv_st, w):
    b, l, d = x.shape
    if shift_st is None:
        shift_st = jnp.zeros((b, d), F32)
    (xr, xw, xk, xv, xa, xg), new_shift = rwkv_mix(x, w["norm_mix"], shift_st, w["mix"])
    m = b * l
    flat = lambda t: t.reshape(m, d)
    r = matmul_ws(flat(xr), w["w_r"], 0)
    k = matmul_ws(flat(xk), w["w_k"], 0)
    v = matmul_ws(flat(xv), w["w_v"], 0)
    wl = lora(flat(xw), w["w1"], w["w2"], w["w0"], "tanh")
    al = lora(flat(xa), w["a1"], w["a2"], w["a0"], "none")
    g = lora(flat(xg), w["g1"], w["g2"], jnp.zeros((d,), F32), "sigmoid")
    chunk = RWKV_CHUNK if l >= RWKV_CHUNK else _round_up(l, SUBLANES)
    n_heads = d // RWKV_N
    if l >= RWKV_CHUNK:
        lb, heads = _pick(l, (512, 256, 128, 64)), _pick(n_heads, (8, 4, 2))
    else:
        lb, heads = chunk, _pick(n_heads, (32, 16, 8, 4, 2))
    to3 = lambda t: t.reshape(b, l, d)
    y, s_new = rwkv_wkv(to3(r), to3(k), to3(v), to3(wl), to3(al), to3(g), w["k_k"], w["k_a"], w["r_k"],
                        w["lnx_w"], w["lnx_b"], wkv_st, chunk, lb, heads)
    x2 = matmul_ws(y.reshape(m, d), w["w_out"], 0, res=x.reshape(m, d))
    return x2.reshape(b, l, d), new_shift, s_new


def _ffn(x, w, layer):
    b, l, d = x.shape
    x2 = x.reshape(b * l, d)
    h = rmsnorm(x2, w["norm"][layer], BF16)
    act = gate_up(h, w["w_gate"], w["w_up"], layer)
    dff = act.shape[1]
    bk = dff // 2 if (dff // 2) % LANES == 0 else dff
    x2 = matmul(act, w["w_down"], layer, res=x2, bn=_pick(d, (512, 256, 128)), bk=bk)
    return x2.reshape(b, l, d)


def _trunk(x, conv_st, ssm_st, shift_st, wkv_st, wts):
    x, conv_new, ssm_new = _gdn_layer(x, conv_st, ssm_st, wts["gdn"])
    x = _ffn(x, wts["ffn"], 0)
    x, shift_new, wkv_new = _rwkv_layer(x, shift_st, wkv_st, wts["rwkv"])
    x = _ffn(x, wts["ffn"], 1)
    b, l, d = x.shape
    y = rmsnorm(x.reshape(b * l, d), wts["norm_final"], F32).reshape(b, l, d)
    return y, conv_new[None], ssm_new[None], shift_new[None], wkv_new[None]


def _pad_to(x, axis, size):
    pad = size - x.shape[axis]
    if pad == 0:
        return x
    widths = [(0, 0)] * x.ndim
    widths[axis] = (0, pad)
    return jnp.pad(x, widths)


def kernel(x_prompt, x_sample, state_gdn_conv, state_gdn_ssm, state_rwkv_shift, state_rwkv_wkv,
           norm_mix, norm_ffn, norm_final,
           gdn_w_in, gdn_conv_w, gdn_a_log, gdn_dt_bias, gdn_norm_w, gdn_w_out,
           rwkv_mix, rwkv_w_r, rwkv_w_k, rwkv_w_v, rwkv_w0, rwkv_w1, rwkv_w2,
           rwkv_a0, rwkv_a1, rwkv_a2, rwkv_g1, rwkv_g2, rwkv_k_k, rwkv_k_a, rwkv_r_k,
           rwkv_lnx_w, rwkv_lnx_b, rwkv_w_out,
           ffn_w_gate, ffn_w_up, ffn_w_down):
    bf = lambda t: t.astype(BF16)
    gate_rank = _round_up(rwkv_g1.shape[-1], LANES)
    wts = {
        "gdn": {
            "norm_mix": norm_mix[0],
            "w_in": _pad_to(gdn_w_in, 2, _round_up(gdn_w_in.shape[2], LANES)),
            "conv_w": gdn_conv_w[0], "a_log": gdn_a_log[0], "dt_bias": gdn_dt_bias[0],
            "norm_w": gdn_norm_w[0], "w_out": bf(gdn_w_out),
        },
        "rwkv": {
            "norm_mix": norm_mix[1], "mix": rwkv_mix[0],
            "w_r": rwkv_w_r, "w_k": rwkv_w_k, "w_v": rwkv_w_v,
            "w0": rwkv_w0[0], "w1": bf(rwkv_w1[0]), "w2": bf(rwkv_w2[0]),
            "a0": rwkv_a0[0], "a1": bf(rwkv_a1[0]), "a2": bf(rwkv_a2[0]),
            "g1": bf(_pad_to(rwkv_g1[0], 1, gate_rank)), "g2": bf(_pad_to(rwkv_g2[0], 0, gate_rank)),
            "k_k": rwkv_k_k[0], "k_a": rwkv_k_a[0], "r_k": rwkv_r_k[0],
            "lnx_w": rwkv_lnx_w[0], "lnx_b": rwkv_lnx_b[0], "w_out": rwkv_w_out,
        },
        "ffn": {"norm": norm_ffn, "w_gate": ffn_w_gate, "w_up": ffn_w_up, "w_down": bf(ffn_w_down)},
        "norm_final": norm_final,
    }
    outs_p = _trunk(x_prompt, None, None, None, None, wts)
    outs_s = _trunk(x_sample, state_gdn_conv[0], state_gdn_ssm[0], state_rwkv_shift[0], state_rwkv_wkv[0], wts)
    return (outs_p[0], outs_s[0]) + tuple(outs_p[1:]) + tuple(outs_s[1:])
```

```python
import functools

import jax
import jax.numpy as jnp
from jax import lax
from jax.experimental import pallas as pl
from jax.experimental.pallas import tpu as pltpu

F32 = jnp.float32
BF16 = jnp.bfloat16
HIGHEST = lax.Precision.HIGHEST

D_MODEL = 4096
DEPTH = 2
GDN_HK = 32
GDN_HV = 64
GDN_DK = 128
GDN_DV = 128
GDN_CONV_W = 4
GDN_CHUNK = 64
RWKV_N = 64
RWKV_CHUNK = 64
RWKV_LNX_EPS = 64e-5
RMS_EPS = 1e-6
L2_EPS = 1e-6

LANES = 128
SUBLANES = 8
VMEM_LIMIT = 56 * 1024 * 1024


def _cparams(sem):
    return pltpu.CompilerParams(dimension_semantics=sem, vmem_limit_bytes=VMEM_LIMIT)


def _pick(n, prefs):
    for p in prefs:
        if n % p == 0:
            return p
    return n


def _dot(a, b, precision=None):
    return jnp.dot(a, b, preferred_element_type=F32, precision=precision)


def _sigmoid(x):
    return 1.0 / (1.0 + jnp.exp(-x))


def _softplus(x):
    return jnp.maximum(x, 0.0) + jnp.log1p(jnp.exp(-jnp.abs(x)))


def _tri_masks(c):
    ii = lax.broadcasted_iota(jnp.int32, (c, c), 0)
    jj = lax.broadcasted_iota(jnp.int32, (c, c), 1)
    return ii >= jj, ii > jj, ii == jj


def _bdot(a, b):
    return lax.dot_general(a, b, (((2,), (1,)), ((0,), (0,))), preferred_element_type=F32)


def _bdot_nt(a, b):
    return lax.dot_general(a, b, (((2,), (2,)), ((0,), (0,))), preferred_element_type=F32)


def _bdot_tn(a, b):
    return lax.dot_general(a, b, (((1,), (1,)), ((0,), (0,))), preferred_element_type=F32)


def _mxu(x, rows):
    return x.astype(BF16) if rows >= 2 * SUBLANES else x


def _chunk_cumsum(x, c):
    incl, _, _ = _tri_masks(c)
    tri = jnp.broadcast_to(jnp.where(incl, 1.0, 0.0).astype(BF16)[None], (x.shape[0], c, c))
    hi = x.astype(BF16)
    rest = x - hi.astype(F32)
    mid = rest.astype(BF16)
    lo = (rest - mid.astype(F32)).astype(BF16)
    return _bdot(tri, hi) + _bdot(tri, mid) + _bdot(tri, lo)


def _pair_masks(c):
    left = lax.broadcasted_iota(jnp.int32, (1, 1, 2 * c), 2) < c
    ii = lax.broadcasted_iota(jnp.int32, (c, 2 * c), 0)
    jj = lax.broadcasted_iota(jnp.int32, (c, 2 * c), 1)
    jm = jnp.where(jj < c, jj, jj - c)
    return left, ii >= jm, ii > jm, ii == jm


def _bd(x, left):
    return jnp.concatenate([jnp.where(left, x, 0.0), jnp.where(left, 0.0, x)], axis=1)


def _pair_inverse(a, c):
    left, _, _, diag = _pair_masks(c)
    x = jnp.where(diag, 1.0, 0.0).astype(F32)[None] + a
    pw = _bdot(_mxu(a, c), _mxu(_bd(a, left), c))
    power = 2
    while power < c:
        if 2 * power < c:
            y = _bdot(_mxu(pw, c), _mxu(jnp.concatenate([_bd(x, left), _bd(pw, left)], axis=2), c))
            x = x + y[:, :, :2 * c]
            pw = y[:, :, 2 * c:]
        else:
            x = x + _bdot(_mxu(pw, c), _mxu(_bd(x, left), c))
        power *= 2
    return x


def _take(x, idx):
    return jnp.stack([x[i] for i in idx])


def _mm_kernel(*refs, nk, has_res):
    if has_res:
        a_ref, b_ref, r_ref, o_ref = refs[:4]
        scratch = refs[4:]
    else:
        a_ref, b_ref, o_ref = refs[:3]
        r_ref = None
        scratch = refs[3:]
    part = _dot(a_ref[...], b_ref[...])

    def finish(total):
        if has_res:
            total = total + r_ref[...]
        o_ref[...] = total.astype(o_ref.dtype)

    if nk == 1:
        finish(part)
        return
    acc_ref = scratch[0]
    k = pl.program_id(2)

    @pl.when(k == 0)
    def _():
        acc_ref[...] = part

    @pl.when(jnp.logical_and(k > 0, k < nk - 1))
    def _():
        acc_ref[...] += part

    @pl.when(k == nk - 1)
    def _():
        finish(acc_ref[...] + part)


def matmul(a, b, layer, res=None, out_dtype=F32, bm=None, bn=None, bk=None):
    m, kdim = a.shape
    n = b.shape[2]
    bm = bm or _pick(m, (1024, 512, 256, 128, 64, 32, 16, 8))
    bk = bk or kdim
    bn = bn or _pick(n, (1024, 512, 256, 128) if (res is None and bk == kdim) else (512, 256, 128))
    nk = kdim // bk
    in_specs = [pl.BlockSpec((bm, bk), lambda i, j, k: (i, k)),
                pl.BlockSpec((None, bk, bn), lambda i, j, k: (layer, k, j))]
    args = [a, b]
    if res is not None:
        in_specs.append(pl.BlockSpec((bm, bn), lambda i, j, k: (i, j)))
        args.append(res)
    scratch = [pltpu.VMEM((bm, bn), F32)] if nk > 1 else []
    return pl.pallas_call(
        functools.partial(_mm_kernel, nk=nk, has_res=res is not None),
        grid=(m // bm, n // bn, nk),
        in_specs=in_specs,
        out_specs=pl.BlockSpec((bm, bn), lambda i, j, k: (i, j)),
        out_shape=jax.ShapeDtypeStruct((m, n), out_dtype),
        scratch_shapes=scratch,
        compiler_params=_cparams(("parallel", "parallel", "arbitrary")),
        name="matmul",
    )(*args)


def _mm_ws_kernel(*refs, has_res):
    if has_res:
        a_ref, w_ref, r_ref, o_ref, wb_ref = refs
    else:
        a_ref, w_ref, o_ref, wb_ref = refs
        r_ref = None

    @pl.when(pl.program_id(1) == 0)
    def _():
        wb_ref[...] = w_ref[...].astype(BF16)

    total = _dot(a_ref[...], wb_ref[...])
    if has_res:
        total = total + r_ref[...]
    o_ref[...] = total.astype(o_ref.dtype)


def matmul_ws(a, w, layer, n_out=None, col0=0, res=None, out_dtype=F32, bn=None):
    m, kdim = a.shape
    n_out = n_out or w.shape[2]
    bm = _pick(m, (1024, 512, 256, 128, 64, 32, 16, 8))
    bn = bn or _pick(n_out, (512, 256, 128))
    assert col0 % bn == 0 and n_out % bn == 0
    off = col0 // bn
    in_specs = [pl.BlockSpec((bm, kdim), lambda j, i: (i, 0)),
                pl.BlockSpec((None, kdim, bn), lambda j, i: (layer, 0, off + j))]
    args = [a, w]
    if res is not None:
        in_specs.append(pl.BlockSpec((bm, bn), lambda j, i: (i, j)))
        args.append(res)
    return pl.pallas_call(
        functools.partial(_mm_ws_kernel, has_res=res is not None),
        grid=(n_out // bn, m // bm),
        in_specs=in_specs,
        out_specs=pl.BlockSpec((bm, bn), lambda j, i: (i, j)),
        out_shape=jax.ShapeDtypeStruct((m, n_out), out_dtype),
        scratch_shapes=[pltpu.VMEM((kdim, bn), BF16)],
        compiler_params=_cparams(("arbitrary", "arbitrary")),
        name="matmul_ws",
    )(*args)


def _gate_up_kernel(h_ref, wg_ref, wu_ref, o_ref, wgb_ref, wub_ref):
    @pl.when(pl.program_id(1) == 0)
    def _():
        wgb_ref[...] = wg_ref[...].astype(BF16)
        wub_ref[...] = wu_ref[...].astype(BF16)

    h = h_ref[...]
    g = _dot(h, wgb_ref[...])
    u = _dot(h, wub_ref[...])
    o_ref[...] = (g * _sigmoid(g) * u).astype(o_ref.dtype)


def gate_up(h, wg, wu, layer):
    m, kdim = h.shape
    n = wg.shape[2]
    bm = _pick(m, (1024, 512, 256, 128, 64, 32, 16, 8))
    bn = _pick(n, (256, 128))
    wspec = pl.BlockSpec((None, kdim, bn), lambda j, i: (layer, 0, j))
    return pl.pallas_call(
        _gate_up_kernel,
        grid=(n // bn, m // bm),
        in_specs=[pl.BlockSpec((bm, kdim), lambda j, i: (i, 0)), wspec, wspec],
        out_specs=pl.BlockSpec((bm, bn), lambda j, i: (i, j)),
        out_shape=jax.ShapeDtypeStruct((m, n), BF16),
        scratch_shapes=[pltpu.VMEM((kdim, bn), BF16), pltpu.VMEM((kdim, bn), BF16)],
        compiler_params=_cparams(("arbitrary", "arbitrary")),
        name="ffn_gate_up",
    )(h, wg, wu)


def _rms_kernel(x_ref, w_ref, o_ref):
    x = x_ref[...]
    y = x * lax.rsqrt(jnp.mean(x * x, axis=-1, keepdims=True) + RMS_EPS)
    o_ref[...] = (y * w_ref[...]).astype(o_ref.dtype)


def rmsnorm(x, w, out_dtype):
    m, d = x.shape
    br = _pick(m, (256, 128, 64, 32, 16, 8))
    return pl.pallas_call(
        _rms_kernel,
        grid=(m // br,),
        in_specs=[pl.BlockSpec((br, d), lambda i: (i, 0)),
                  pl.BlockSpec((1, d), lambda i: (0, 0))],
        out_specs=pl.BlockSpec((br, d), lambda i: (i, 0)),
        out_shape=jax.ShapeDtypeStruct((m, d), out_dtype),
        compiler_params=_cparams(("parallel",)),
        name="rmsnorm",
    )(x, w.reshape(1, d))


def _gdn_prep_kernel(x_ref, buf_ref, cw_ref, o_ref, win_ref, *, lb, n_q_blocks, n_qk_blocks, q_scale):
    cblk = pl.program_id(1)
    l = pl.program_id(2)
    halo = GDN_CONV_W - 1
    top = SUBLANES

    @pl.when(l == 0)
    def _():
        win_ref[:, top - halo:top, :] = buf_ref[...]

    win_ref[:, top:top + lb, :] = x_ref[...]
    acc = win_ref[:, top - halo:top - halo + lb, :] * cw_ref[0:1, :]
    for j in range(1, GDN_CONV_W):
        acc = acc + win_ref[:, top - halo + j:top - halo + j + lb, :] * cw_ref[j:j + 1, :]
    carry = win_ref[:, top + lb - halo:top + lb, :]
    win_ref[:, top - halo:top, :] = carry
    y = acc * _sigmoid(acc)

    @pl.when(cblk >= n_qk_blocks)
    def _():
        o_ref[...] = y

    @pl.when(cblk < n_qk_blocks)
    def _():
        scale = jnp.where(cblk < n_q_blocks, q_scale, 1.0).astype(F32)
        for s in range(y.shape[-1] // GDN_DK):
            ys = y[:, :, s * GDN_DK:(s + 1) * GDN_DK]
            inv = lax.rsqrt(jnp.sum(ys * ys, axis=-1, keepdims=True) + L2_EPS) * scale
            o_ref[:, :, s * GDN_DK:(s + 1) * GDN_DK] = ys * inv


def gdn_prep(proj, conv_buf, conv_w):
    b, l, _ = proj.shape
    kdim = GDN_HK * GDN_DK
    conv_dim = conv_w.shape[1]
    cb = _pick(kdim, (2048, 1024, 512, 256, 128))
    lb = _pick(l, (256, 128, 64, 32, 16, 8))
    bb = _pick(b, (16, 8, 4, 2, 1)) if l < SUBLANES else 1
    kern = functools.partial(_gdn_prep_kernel, lb=lb, n_q_blocks=kdim // cb, n_qk_blocks=2 * kdim // cb,
                             q_scale=float(GDN_DK) ** -0.5)
    return pl.pallas_call(
        kern,
        grid=(b // bb, conv_dim // cb, l // lb),
        in_specs=[pl.BlockSpec((bb, lb, cb), lambda i, c, t: (i, t, c)),
                  pl.BlockSpec((bb, GDN_CONV_W - 1, cb), lambda i, c, t: (i, 0, c)),
                  pl.BlockSpec((GDN_CONV_W, cb), lambda i, c, t: (0, c))],
        out_specs=pl.BlockSpec((bb, lb, cb), lambda i, c, t: (i, t, c)),
        out_shape=jax.ShapeDtypeStruct((b, l, conv_dim), F32),
        scratch_shapes=[pltpu.VMEM((bb, SUBLANES + lb, cb), F32)],
        compiler_params=_cparams(("parallel", "parallel", "arbitrary")),
        name="gdn_prep",
    )(proj, conv_buf, conv_w)


def _in_proj_conv_kernel(a_ref, w_ref, buf_ref, cw_ref, o_ref, st_ref, wb_ref, win_ref, *,
                         bm, bn, blocks_per_seq, n_q_blocks, n_qk_blocks, q_scale):
    j = pl.program_id(0)
    i = pl.program_id(1)
    halo = GDN_CONV_W - 1
    top = SUBLANES
    mxu_cols = 2 * LANES

    @pl.when(i == 0)
    def _():
        wb_ref[...] = w_ref[...].astype(BF16)

    @pl.when(i % blocks_per_seq == 0)
    def _():
        win_ref[top - halo:top, :] = buf_ref[0]

    scale = jnp.where(j < n_q_blocks, q_scale, 1.0).astype(F32)
    normed = j < n_qk_blocks
    a = a_ref[...]
    for c0 in range(0, bn, mxu_cols):
        cols = slice(c0, c0 + mxu_cols)
        win_ref[top:top + bm, cols] = _dot(a, wb_ref[:, cols])
        acc = win_ref[top - halo:top - halo + bm, cols] * cw_ref[0:1, cols]
        for tap in range(1, GDN_CONV_W):
            acc = acc + win_ref[top - halo + tap:top - halo + tap + bm, cols] * cw_ref[tap:tap + 1, cols]
        y = acc * _sigmoid(acc)
        for s0 in range(0, mxu_cols, GDN_DK):
            ys = y[:, s0:s0 + GDN_DK]
            inv = lax.rsqrt(jnp.sum(ys * ys, axis=-1, keepdims=True) + L2_EPS) * scale
            o_ref[:, c0 + s0:c0 + s0 + GDN_DK] = ys * jnp.where(normed, inv, 1.0)
    last = win_ref[top + bm - halo:top + bm, :]
    st_ref[0] = last
    win_ref[top - halo:top, :] = last


def gdn_in_proj_conv(h, w_in, conv_buf, conv_w, b, l):
    m, kdim = h.shape
    conv_dim = conv_w.shape[1]
    key_dim = GDN_HK * GDN_DK
    bm = _pick(l, (1024, 512, 256, 128, 64))
    bn = _pick(key_dim, (512, 256))
    bps = l // bm
    kern = functools.partial(_in_proj_conv_kernel, bm=bm, bn=bn, blocks_per_seq=bps, n_q_blocks=key_dim // bn,
                             n_qk_blocks=2 * key_dim // bn, q_scale=float(GDN_DK) ** -0.5)
    state_spec = pl.BlockSpec((1, GDN_CONV_W - 1, bn), lambda j, i: (i // bps, 0, j))
    return pl.pallas_call(
        kern,
        grid=(conv_dim // bn, m // bm),
        in_specs=[pl.BlockSpec((bm, kdim), lambda j, i: (i, 0)),
                  pl.BlockSpec((None, kdim, bn), lambda j, i: (0, 0, j)),
                  state_spec,
                  pl.BlockSpec((GDN_CONV_W, bn), lambda j, i: (0, j))],
        out_specs=[pl.BlockSpec((bm, bn), lambda j, i: (i, j)), state_spec],
        out_shape=[jax.ShapeDtypeStruct((m, conv_dim), F32),
                   jax.ShapeDtypeStruct((b, GDN_CONV_W - 1, conv_dim), F32)],
        scratch_shapes=[pltpu.VMEM((kdim, bn), BF16), pltpu.VMEM((SUBLANES + bm, bn), F32)],
        compiler_params=_cparams(("arbitrary", "arbitrary")),
        name="gdn_in_proj_conv",
    )(h, w_in, conv_buf, conv_w)


def _gdn_gates_kernel(ba_ref, alog_ref, dtb_ref, beta_ref, gc_ref, pad_ref, *, lb, lpb, chunk):
    hv = GDN_HV
    ba = ba_ref[0]
    beta = _sigmoid(ba[:, :hv])
    g = -jnp.exp(alog_ref[...]) * _softplus(ba[:, hv:2 * hv] + dtb_ref[...])
    if lpb != lb:
        pad_ref[...] = jnp.zeros_like(pad_ref)
        pad_ref[0:lb, :] = g
        g = pad_ref[...]
        pad_ref[0:lb, :] = beta
        beta = pad_ref[...]
    ii = lax.broadcasted_iota(jnp.int32, (lpb, lpb), 0)
    jj = lax.broadcasted_iota(jnp.int32, (lpb, lpb), 1)
    same_chunk_causal = jnp.logical_and(ii >= jj, ii // chunk == jj // chunk)
    gc = _dot(jnp.where(same_chunk_causal, 1.0, 0.0).astype(F32), g, HIGHEST)
    beta_ref[0] = beta
    gc_ref[0] = gc


def gdn_gates(ba, a_log, dt_bias, chunk, lp):
    b, l, w = ba.shape
    lb = _pick(l, (256, 128, 64, 32, 16, 8))
    lpb = lb if lp == l else lp
    kern = functools.partial(_gdn_gates_kernel, lb=lb, lpb=lpb, chunk=chunk)
    out = jax.ShapeDtypeStruct((b, lp, GDN_HV), F32)
    return pl.pallas_call(
        kern,
        grid=(b, l // lb),
        in_specs=[pl.BlockSpec((1, lb, w), lambda i, t: (i, t, 0)),
                  pl.BlockSpec((1, GDN_HV), lambda i, t: (0, 0)),
                  pl.BlockSpec((1, GDN_HV), lambda i, t: (0, 0))],
        out_specs=[pl.BlockSpec((1, lpb, GDN_HV), lambda i, t: (i, t, 0)),
                   pl.BlockSpec((1, lpb, GDN_HV), lambda i, t: (i, t, 0))],
        out_shape=[out, out],
        scratch_shapes=[pltpu.VMEM((lpb, GDN_HV), F32)],
        compiler_params=_cparams(("parallel", "parallel")),
        name="gdn_gates",
    )(ba, a_log.reshape(1, GDN_HV), dt_bias.reshape(1, GDN_HV))


def _gdn_delta_kernel(*refs, l_real, lb, chunk, hb, has_s0):
    if has_s0:
        q_ref, k_ref, v_ref, z_ref, beta_ref, gc_ref, gct_ref, nw_ref, s0_ref = refs[:9]
        rest = refs[9:]
    else:
        q_ref, k_ref, v_ref, z_ref, beta_ref, gc_ref, gct_ref, nw_ref = refs[:8]
        s0_ref = None
        rest = refs[8:]
    o_ref, so_ref, s_scr, padk_scr, padv_scr = rest
    hblk = pl.program_id(1)
    t = pl.program_id(2)
    nt = pl.num_programs(2)
    c = chunk
    nc = lb // c
    padded = l_real < lb
    rep = GDN_HV // GDN_HK
    kh = hb // rep
    pairs = hb // 2
    left_c, incl, strict, _ = _pair_masks(c)

    @pl.when(t == 0)
    def _():
        if has_s0:
            s_scr[...] = s0_ref[0]
        else:
            s_scr[...] = jnp.zeros_like(s_scr)

    def load(ref, pad_scr):
        if padded:
            pad_scr[...] = jnp.zeros_like(pad_scr)
            pad_scr[0:l_real, :] = ref[0]
            return pad_scr[...]
        return ref[0]

    q2 = load(q_ref, padk_scr)
    k2 = load(k_ref, padk_scr)
    v2 = load(v_ref, padv_scr)
    beta_t = beta_ref[0]
    gc_t = gc_ref[0]
    lane = lax.broadcasted_iota(jnp.int32, (lb, GDN_HV), 1)

    q3, k3, qk2, kk2 = [], [], [], []
    for p in range(kh):
        qp = q2[:, p * GDN_DK:(p + 1) * GDN_DK].reshape(nc, c, GDN_DK)
        kp = k2[:, p * GDN_DK:(p + 1) * GDN_DK].reshape(nc, c, GDN_DK)
        q3.append(qp)
        k3.append(kp)
        kp_twice = _mxu(jnp.concatenate([kp, kp], axis=1), c)
        qk2.append(_bdot_nt(_mxu(qp, c), kp_twice))
        kk2.append(_bdot_nt(_mxu(kp, c), kp_twice))

    def column(tile, head):
        return jnp.sum(jnp.where(lane == head, tile, 0.0), axis=-1, keepdims=True).reshape(nc, c, 1)

    zero = jnp.zeros((nc, c, GDN_DV + GDN_DK), F32)
    m_l, rhs_l, ai_l = [], [], []
    wq_l, kg_l, gl_l = [], [], []
    for pp in range(pairs):
        kp, qp = k3[2 * pp // rep], q3[2 * pp // rep]
        beta = [column(beta_t, hblk * hb + 2 * pp + e) for e in range(2)]
        gcol = [column(gc_t, hblk * hb + 2 * pp + e) for e in range(2)]
        grow = gct_ref[0, pp]
        diff = jnp.where(left_c, gcol[0], gcol[1]) - grow
        decay = jnp.where(incl, jnp.exp(jnp.where(incl, diff, 0.0)), 0.0)
        ai_l.append(_mxu(qk2[2 * pp // rep] * decay, c))
        m_l.append(jnp.where(strict, kk2[2 * pp // rep] * (jnp.where(left_c, beta[0], beta[1]) * decay), 0.0))
        blocks = []
        for e in range(2):
            j = 2 * pp + e
            eg = jnp.exp(gcol[e])
            g_last = gcol[e][:, c - 1:c, :]
            vj = v2[:, j * GDN_DV:(j + 1) * GDN_DV].reshape(nc, c, GDN_DV)
            blocks.append(jnp.concatenate([vj * beta[e], kp * (beta[e] * eg)], axis=-1))
            wq_l.append(qp * eg)
            kg_l.append(_mxu(kp * jnp.exp(g_last - gcol[e]), c))
            gl_l.append(jnp.exp(g_last))
        rhs_l.append(jnp.concatenate([jnp.concatenate([blocks[0], zero], axis=-1),
                                      jnp.concatenate([zero, blocks[1]], axis=-1)], axis=1))
    cat = lambda xs: jnp.concatenate(xs, axis=0)
    tinv = _mxu(_pair_inverse(-cat(m_l), c), c)
    uw = _bdot(tinv, _mxu(cat(rhs_l), c))
    a_intra = cat(ai_l)
    width = GDN_DV + GDN_DK
    u_l, wq_rows = [], []
    for pp in range(pairs):
        for e in range(2):
            part = uw[pp * nc:(pp + 1) * nc, :, e * width:(e + 1) * width]
            u_l.append(part[:, :, :GDN_DV])
            wq_rows.append(_mxu(jnp.concatenate([part[:, :, GDN_DV:], wq_l[2 * pp + e]], axis=1), c))

    s = s_scr[...]
    zero_v = jnp.zeros((pairs, c, GDN_DV), F32)
    for ci in range(nc):
        pick = lambda xs: jnp.stack([x[ci] for x in xs])
        ws_qs = _bdot(pick(wq_rows), _mxu(s, c))
        v_new = pick(u_l) - ws_qs[:, :c]
        v_newb = _mxu(v_new, c)
        v_even = jnp.stack([v_new[2 * pp] for pp in range(pairs)])
        v_odd = jnp.stack([v_new[2 * pp + 1] for pp in range(pairs)])
        v_bd = jnp.concatenate([jnp.concatenate([v_even, zero_v], axis=-1),
                                jnp.concatenate([zero_v, v_odd], axis=-1)], axis=1)
        av = _bdot(_take(a_intra, [pp * nc + ci for pp in range(pairs)]), _mxu(v_bd, c))
        o = ws_qs[:, c:] + jnp.stack([av[j // 2][:, (j % 2) * GDN_DV:(j % 2 + 1) * GDN_DV] for j in range(hb)])
        s = s * pick(gl_l) + _bdot_tn(pick(kg_l), v_newb)
        o = o * lax.rsqrt(jnp.mean(o * o, axis=-1, keepdims=True) + RMS_EPS) * nw_ref[...]
        for j in range(hb):
            vslab = slice(j * GDN_DV, (j + 1) * GDN_DV)
            if padded:
                z = z_ref[0, :, vslab]
                o_ref[0, :, vslab] = (o[j][0:l_real] * (z * _sigmoid(z))).astype(o_ref.dtype)
            else:
                z = z_ref[0, ci * c:(ci + 1) * c, vslab]
                o_ref[0, ci * c:(ci + 1) * c, vslab] = (o[j] * (z * _sigmoid(z))).astype(o_ref.dtype)
    s_scr[...] = s

    @pl.when(t == nt - 1)
    def _():
        so_ref[0] = s


def gdn_delta(qkv, zsrc, z_col0, beta, gc, gct, norm_w, s0, l_real, chunk, lb, hb):
    b = qkv.shape[0]
    lp = beta.shape[1]
    lrows = l_real if l_real < lb else lb
    nt = lp // lb
    rep = GDN_HV // GDN_HK
    kh = hb // rep
    k_base = GDN_HK // kh
    v_base = (2 * GDN_HK * GDN_DK) // (hb * GDN_DV)
    z_base = z_col0 // (hb * GDN_DV)
    in_specs = [
        pl.BlockSpec((1, lrows, kh * GDN_DK), lambda i, h, t: (i, t, h)),
        pl.BlockSpec((1, lrows, kh * GDN_DK), lambda i, h, t: (i, t, k_base + h)),
        pl.BlockSpec((1, lrows, hb * GDN_DV), lambda i, h, t: (i, t, v_base + h)),
        pl.BlockSpec((1, lrows, hb * GDN_DV), lambda i, h, t: (i, t, z_base + h)),
        pl.BlockSpec((1, lb, GDN_HV), lambda i, h, t: (i, t, 0)),
        pl.BlockSpec((1, lb, GDN_HV), lambda i, h, t: (i, t, 0)),
        pl.BlockSpec((1, hb // 2, lb // chunk, 1, 2 * chunk), lambda i, h, t: (i, h, t, 0, 0)),
        pl.BlockSpec((1, GDN_DV), lambda i, h, t: (0, 0)),
    ]
    args = [qkv, qkv, qkv, zsrc, beta, gc, gct, norm_w.reshape(1, GDN_DV)]
    if s0 is not None:
        in_specs.append(pl.BlockSpec((1, hb, GDN_DK, GDN_DV), lambda i, h, t: (i, h, 0, 0)))
        args.append(s0)
    kern = functools.partial(_gdn_delta_kernel, l_real=l_real, lb=lb, chunk=chunk, hb=hb, has_s0=s0 is not None)
    return pl.pallas_call(
        kern,
        grid=(b, GDN_HV // hb, nt),
        in_specs=in_specs,
        out_specs=[pl.BlockSpec((1, lrows, hb * GDN_DV), lambda i, h, t: (i, t, h)),
                   pl.BlockSpec((1, hb, GDN_DK, GDN_DV), lambda i, h, t: (i, h, 0, 0))],
        out_shape=[jax.ShapeDtypeStruct((b, l_real, GDN_HV * GDN_DV), BF16),
                   jax.ShapeDtypeStruct((b, GDN_HV, GDN_DK, GDN_DV), F32)],
        scratch_shapes=[pltpu.VMEM((hb, GDN_DK, GDN_DV), F32),
                        pltpu.VMEM((lb, kh * GDN_DK), F32),
                        pltpu.VMEM((lb, hb * GDN_DV), F32)],
        compiler_params=_cparams(("parallel", "parallel", "arbitrary")),
        name="gdn_delta",
    )(*args)


def _rwkv_mix_kernel(x_ref, nw_ref, shift_ref, mix_ref, o0, o1, o2, o3, o4, o5, last_ref, win_ref, *, lb):
    t = pl.program_id(1)
    top = SUBLANES

    @pl.when(t == 0)
    def _():
        win_ref[top - 1:top, :] = shift_ref[0]

    x = x_ref[0]
    h = x * lax.rsqrt(jnp.mean(x * x, axis=-1, keepdims=True) + RMS_EPS) * nw_ref[...]
    win_ref[top:top + lb, :] = h
    prev = win_ref[top - 1:top - 1 + lb, :]
    last = win_ref[top + lb - 1:top + lb, :]
    win_ref[top - 1:top, :] = last
    last_ref[0] = last
    xx = prev - h
    for i, o in enumerate((o0, o1, o2, o3, o4, o5)):
        o[0] = (h + xx * mix_ref[i:i + 1, :]).astype(o.dtype)


def rwkv_mix(x, norm_w, shift, mix):
    b, l, d = x.shape
    lb = _pick(l, (128, 64, 32, 16, 8))
    kern = functools.partial(_rwkv_mix_kernel, lb=lb)
    tok = pl.BlockSpec((1, lb, d), lambda i, t: (i, t, 0))
    row = pl.BlockSpec((1, 1, d), lambda i, t: (i, 0, 0))
    outs = pl.pallas_call(
        kern,
        grid=(b, l // lb),
        in_specs=[tok, pl.BlockSpec((1, d), lambda i, t: (0, 0)), row,
                  pl.BlockSpec((6, d), lambda i, t: (0, 0))],
        out_specs=[tok] * 6 + [row],
        out_shape=[jax.ShapeDtypeStruct((b, l, d), BF16)] * 6 + [jax.ShapeDtypeStruct((b, 1, d), F32)],
        scratch_shapes=[pltpu.VMEM((SUBLANES + lb, d), F32)],
        compiler_params=_cparams(("parallel", "arbitrary")),
        name="rwkv_mix",
    )(x, norm_w.reshape(1, d), shift.reshape(b, 1, d), mix)
    return outs[:6], outs[6].reshape(b, d)


def _lora_kernel(x_ref, w1_ref, w2_ref, b_ref, o_ref, *, act):
    hmid = _dot(x_ref[...], w1_ref[...])
    if act == "tanh":
        hmid = jnp.tanh(hmid)
    elif act == "sigmoid":
        hmid = _sigmoid(hmid)
    o_ref[...] = b_ref[...] + _dot(hmid.astype(BF16), w2_ref[...])


def lora(x, w1, w2, bias, act):
    m, d = x.shape
    r = w1.shape[1]
    n = w2.shape[1]
    bm = _pick(m, (256, 128, 64, 32, 16, 8))
    return pl.pallas_call(
        functools.partial(_lora_kernel, act=act),
        grid=(m // bm,),
        in_specs=[pl.BlockSpec((bm, d), lambda i: (i, 0)),
                  pl.BlockSpec((d, r), lambda i: (0, 0)),
                  pl.BlockSpec((r, n), lambda i: (0, 0)),
                  pl.BlockSpec((1, n), lambda i: (0, 0))],
        out_specs=pl.BlockSpec((bm, n), lambda i: (i, 0)),
        out_shape=jax.ShapeDtypeStruct((m, n), F32),
        compiler_params=_cparams(("parallel",)),
        name="rwkv_lora",
    )(x, w1, w2, bias.reshape(1, n))


def _rwkv_pair_kernel(*refs, l_real, lb, chunk, has_s0):
    n_in = 12 if has_s0 else 11
    (r_ref, k_ref, v_ref, w_ref, a_ref, g_ref, kk_ref, ka_ref, rk_ref, lnw_ref, lnb_ref) = refs[:11]
    s0_ref = refs[11] if has_s0 else None
    o_ref, so_ref, s_scr, pad_scr = refs[n_in:]
    t = pl.program_id(2)
    nt = pl.num_programs(2)
    n = RWKV_N
    c = chunk
    nc = lb // c
    width = r_ref.shape[-1]
    pairs = width // (2 * n)
    padded = l_real < lb
    left_n = lax.broadcasted_iota(jnp.int32, (1, 1, 2 * n), 2) < n
    _, incl, strict, _ = _pair_masks(c)
    rr = lax.broadcasted_iota(jnp.int32, (2 * n, 2 * n), 0) < n
    cc = lax.broadcasted_iota(jnp.int32, (2 * n, 2 * n), 1) < n
    same_head = rr == cc
    bd = _bd

    def head_sum(x):
        sl = jnp.sum(jnp.where(left_n, x, 0.0), axis=-1, keepdims=True)
        sr = jnp.sum(jnp.where(left_n, 0.0, x), axis=-1, keepdims=True)
        return jnp.where(left_n, sl, sr)

    @pl.when(t == 0)
    def _():
        for p in range(pairs):
            if has_s0:
                zero = jnp.zeros((n, n), F32)
                top = jnp.concatenate([s0_ref[0, 2 * p], zero], axis=1)
                bot = jnp.concatenate([zero, s0_ref[0, 2 * p + 1]], axis=1)
                s_scr[p] = jnp.concatenate([top, bot], axis=0)
            else:
                s_scr[p] = jnp.zeros((2 * n, 2 * n), F32)

    def load(ref):
        if padded:
            pad_scr[...] = jnp.zeros_like(pad_scr)
            pad_scr[0:l_real, :] = ref[0]
            return pad_scr[...]
        return ref[0]

    r2, k2, v2, w2, a2, g2 = (load(ref) for ref in (r_ref, k_ref, v_ref, w_ref, a_ref, g_ref))
    lw2 = -jnp.exp(-_softplus(-w2) - 0.5)
    if padded:
        rows = lax.broadcasted_iota(jnp.int32, lw2.shape, 0)
        lw2 = jnp.where(rows < l_real, lw2, 0.0)
    a_sig = _sigmoid(a2)
    kmod = k2 * (1.0 + (a_sig - 1.0) * ka_ref[...])
    cl2 = _chunk_cumsum(lw2.reshape(nc, c, width), c).reshape(lb, width)

    def split(x2):
        x3 = x2.reshape(nc, c, width)
        return jnp.concatenate([x3[:, :, p * 2 * n:(p + 1) * 2 * n] for p in range(pairs)], axis=0)

    def per_pair(p_ref):
        return jnp.concatenate([jnp.broadcast_to(p_ref[:, p * 2 * n:(p + 1) * 2 * n][None], (nc, 1, 2 * n))
                                for p in range(pairs)], axis=0)

    kk = split(k2 * kk_ref[...])
    kk = kk * lax.rsqrt(head_sum(kk * kk) + L2_EPS)
    ep = split(jnp.exp(cl2))
    em = split(jnp.exp(-cl2))
    a_h = -kk * split(jnp.exp(cl2 - lw2))
    b_h = kk * split(a_sig) * em
    r = split(r2)
    k = split(kmod)
    v = split(v2)
    r_h = r * ep
    k_h = k * em
    pc = ep[:, c - 1:c, :]

    amat = _bdot_nt(_mxu(jnp.concatenate([a_h, r_h], axis=1), c),
                    _mxu(jnp.concatenate([bd(k_h, left_n), bd(b_h, left_n)], axis=1), c))
    a_ak = jnp.where(strict, amat[:, :c, :2 * c], 0.0)
    a_ab = jnp.where(strict, amat[:, :c, 2 * c:], 0.0)
    a_rk = jnp.where(incl, amat[:, c:, :2 * c], 0.0)
    a_rb = jnp.where(incl, amat[:, c:, 2 * c:], 0.0)

    tinv = _mxu(_pair_inverse(a_ab, c), c)

    vbd = bd(v, left_n)
    akv = _bdot(_mxu(a_ak, c), _mxu(vbd, c))
    wu = _bdot(tinv, _mxu(jnp.concatenate([bd(a_h, left_n), bd(akv, left_n)], axis=2), c))
    wmat = wu[:, :, :2 * n]
    ut = wu[:, :, 2 * n:]
    top = jnp.concatenate([bd(wmat, left_n), bd(ut, left_n)], axis=2)
    bot = jnp.concatenate([jnp.zeros_like(vbd), vbd], axis=2)
    ry = _bdot(_mxu(jnp.concatenate([a_rb, a_rk], axis=2), c), _mxu(jnp.concatenate([top, bot], axis=1), c))
    rt = r_h + ry[:, :, :2 * n]
    yt = ry[:, :, 2 * n:]
    bt = b_h * pc
    btb = _mxu(bt, c)
    hmat = _bdot_tn(_mxu(jnp.concatenate([ut, v], axis=1), c),
                    _mxu(jnp.concatenate([bt, k_h * pc], axis=1), c))
    hmat = jnp.where(same_head, hmat, 0.0)
    rw = _mxu(jnp.concatenate([rt, wmat], axis=1), c)

    s = s_scr[...]
    ys = [None] * (pairs * nc)
    for ci in range(nc):
        idx = [p * nc + ci for p in range(pairs)]
        ys_sw = _bdot_nt(_take(rw, idx), _mxu(s, c))
        y_c = ys_sw[:, :c] + _take(yt, idx)
        swt = _mxu(ys_sw[:, c:], c)
        s = s * _take(pc, idx) + jnp.where(same_head, _bdot_tn(swt, _take(btb, idx)), 0.0) + _take(hmat, idx)
        for j, i in enumerate(idx):
            ys[i] = y_c[j]
    s_scr[...] = s
    y = jnp.stack(ys)

    yc = y - head_sum(y) * (1.0 / n)
    var = head_sum(yc * yc) * (1.0 / n)
    yn = yc * lax.rsqrt(var + RWKV_LNX_EPS) * per_pair(lnw_ref) + per_pair(lnb_ref)
    yn = yn + head_sum(r * k * per_pair(rk_ref)) * v
    out = (yn * split(g2)).astype(o_ref.dtype)
    for p in range(pairs):
        rows = out[p * nc:(p + 1) * nc].reshape(lb, 2 * n)
        if padded:
            o_ref[0, :, p * 2 * n:(p + 1) * 2 * n] = rows[0:l_real]
        else:
            o_ref[0, :, p * 2 * n:(p + 1) * 2 * n] = rows

    @pl.when(t == nt - 1)
    def _():
        for p in range(pairs):
            so_ref[0, 2 * p] = s[p][:n, :n]
            so_ref[0, 2 * p + 1] = s[p][n:, n:]


def rwkv_wkv(r, k, v, w, a, g, k_k, k_a, r_k, lnx_w, lnx_b, s0, chunk, lb, heads):
    b, l, d = r.shape
    n = RWKV_N
    h = d // n
    width = heads * n
    lrows = l if l < lb else lb
    nt = max(1, l // lb)
    tok = pl.BlockSpec((1, lrows, width), lambda i, p, t: (i, t, p))
    par = pl.BlockSpec((1, width), lambda i, p, t: (0, p))
    st = pl.BlockSpec((1, heads, n, n), lambda i, p, t: (i, p, 0, 0))
    in_specs = [tok] * 6 + [par] * 5
    args = [r, k, v, w, a, g, k_k.reshape(1, d), k_a.reshape(1, d), r_k.reshape(1, d),
            lnx_w.reshape(1, d), lnx_b.reshape(1, d)]
    if s0 is not None:
        in_specs.append(st)
        args.append(s0)
    kern = functools.partial(_rwkv_pair_kernel, l_real=l, lb=lb, chunk=chunk, has_s0=s0 is not None)
    return pl.pallas_call(
        kern,
        grid=(b, h // heads, nt),
        in_specs=in_specs,
        out_specs=[tok, st],
        out_shape=[jax.ShapeDtypeStruct((b, l, d), BF16),
                   jax.ShapeDtypeStruct((b, h, n, n), F32)],
        scratch_shapes=[pltpu.VMEM((heads // 2, 2 * n, 2 * n), F32), pltpu.VMEM((lb, width), F32)],
        compiler_params=_cparams(("parallel", "parallel", "arbitrary")),
        name="rwkv_wkv",
    )(*args)


def _round_up(x, m):
    return -(-x // m) * m


def _gdn_layer(x, conv_st, ssm_st, w):
    b, l, d = x.shape
    conv_dim = 2 * GDN_HK * GDN_DK + GDN_HV * GDN_DV
    x2 = x.reshape(b * l, d)
    h = rmsnorm(x2, w["norm_mix"], BF16)
    vdim = GDN_HV * GDN_DV
    main_cols = conv_dim + vdim
    ba_cols = _round_up(2 * GDN_HV, LANES)
    ba = matmul_ws(h, w["w_in"], 0, n_out=ba_cols, col0=main_cols, bn=LANES).reshape(b, l, -1)
    if conv_st is None:
        conv_st = jnp.zeros((b, GDN_CONV_W - 1, conv_dim), F32)
    if l >= GDN_CHUNK:
        qkv, new_conv = gdn_in_proj_conv(h, w["w_in"], conv_st, w["conv_w"], b, l)
        qkv = qkv.reshape(b, l, conv_dim)
        zsrc, z_col0 = matmul_ws(h, w["w_in"], 0, n_out=vdim, col0=conv_dim).reshape(b, l, vdim), 0
    else:
        proj = matmul_ws(h, w["w_in"], 0, n_out=main_cols).reshape(b, l, -1)
        new_conv = proj[:, l - (GDN_CONV_W - 1):, :conv_dim]
        qkv = gdn_prep(proj, conv_st, w["conv_w"])
        zsrc, z_col0 = proj, conv_dim
    chunk = GDN_CHUNK if l >= GDN_CHUNK else _round_up(l, SUBLANES)
    lp = _round_up(l, chunk)
    beta, gc = gdn_gates(ba, w["a_log"], w["dt_bias"], chunk, lp)
    gct = jnp.swapaxes(gc, 1, 2).reshape(b, GDN_HV // 2, 2, lp // chunk, chunk)
    gct = jnp.transpose(gct, (0, 1, 3, 2, 4)).reshape(b, GDN_HV // 2, lp // chunk, 1, 2 * chunk)
    if l >= GDN_CHUNK:
        lb, hb = _pick(lp, (512, 256, 128, 64)), _pick(GDN_HV, (8, 4, 2))
    else:
        lb, hb = lp, _pick(GDN_HV, (64, 32, 16, 8, 4, 2))
    o, s_new = gdn_delta(qkv, zsrc, z_col0, beta, gc, gct, w["norm_w"], ssm_st, l, chunk, lb, hb)
    x2 = matmul(o.reshape(b * l, -1), w["w_out"], 0, res=x2, bk=_pick(o.shape[-1], (4096,)))
    return x2.reshape(b, l, d), new_conv, s_new


def _rwkv_layer(x, shift_st, wkv_st, w):
    b, l, d = x.shape
    if shift_st is None:
        shift_st = jnp.zeros((b, d), F32)
    (xr, xw, xk, xv, xa, xg), new_shift = rwkv_mix(x, w["norm_mix"], shift_st, w["mix"])
    m = b * l
    flat = lambda t: t.reshape(m, d)
    r = matmul_ws(flat(xr), w["w_r"], 0)
    k = matmul_ws(flat(xk), w["w_k"], 0)
    v = matmul_ws(flat(xv), w["w_v"], 0)
    wl = lora(flat(xw), w["w1"], w["w2"], w["w0"], "tanh")
    al = lora(flat(xa), w["a1"], w["a2"], w["a0"], "none")
    g = lora(flat(xg), w["g1"], w["g2"], jnp.zeros((d,), F32), "sigmoid")
    chunk = RWKV_CHUNK if l >= RWKV_CHUNK else _round_up(l, SUBLANES)
    n_heads = d // RWKV_N
    if l >= RWKV_CHUNK:
        lb, heads = _pick(l, (512, 256, 128, 64)), _pick(n_heads, (8, 4, 2))
    else:
        lb, heads = chunk, _pick(n_heads, (64, 32, 16, 8, 4, 2))
    to3 = lambda t: t.reshape(b, l, d)
    y, s_new = rwkv_wkv(to3(r), to3(k), to3(v), to3(wl), to3(al), to3(g), w["k_k"], w["k_a"], w["r_k"],
                        w["lnx_w"], w["lnx_b"], wkv_st, chunk, lb, heads)
    x2 = matmul_ws(y.reshape(m, d), w["w_out"], 0, res=x.reshape(m, d))
    return x2.reshape(b, l, d), new_shift, s_new


def _ffn(x, w, layer):
    b, l, d = x.shape
    x2 = x.reshape(b * l, d)
    h = rmsnorm(x2, w["norm"][layer], BF16)
    act = gate_up(h, w["w_gate"], w["w_up"], layer)
    dff = act.shape[1]
    bk = dff // 2 if (dff // 2) % LANES == 0 else dff
    x2 = matmul(act, w["w_down"], layer, res=x2, bn=_pick(d, (512, 256, 128)), bk=bk)
    return x2.reshape(b, l, d)


def _trunk(x, conv_st, ssm_st, shift_st, wkv_st, wts):
    x, conv_new, ssm_new = _gdn_layer(x, conv_st, ssm_st, wts["gdn"])
    x = _ffn(x, wts["ffn"], 0)
    x, shift_new, wkv_new = _rwkv_layer(x, shift_st, wkv_st, wts["rwkv"])
    x = _ffn(x, wts["ffn"], 1)
    b, l, d = x.shape
    y = rmsnorm(x.reshape(b * l, d), wts["norm_final"], F32).reshape(b, l, d)
    return y, conv_new[None], ssm_new[None], shift_new[None], wkv_new[None]


def _pad_to(x, axis, size):
    pad = size - x.shape[axis]
    if pad == 0:
        return x
    widths = [(0, 0)] * x.ndim
    widths[axis] = (0, pad)
    return jnp.pad(x, widths)


def kernel(x_prompt, x_sample, state_gdn_conv, state_gdn_ssm, state_rwkv_shift, state_rwkv_wkv,
           norm_mix, norm_ffn, norm_final,
           gdn_w_in, gdn_conv_w, gdn_a_log, gdn_dt_bias, gdn_norm_w, gdn_w_out,
           rwkv_mix, rwkv_w_r, rwkv_w_k, rwkv_w_v, rwkv_w0, rwkv_w1, rwkv_w2,
           rwkv_a0, rwkv_a1, rwkv_a2, rwkv_g1, rwkv_g2, rwkv_k_k, rwkv_k_a, rwkv_r_k,
           rwkv_lnx_w, rwkv_lnx_b, rwkv_w_out,
           ffn_w_gate, ffn_w_up, ffn_w_down):
    bf = lambda t: t.astype(BF16)
    gate_rank = _round_up(rwkv_g1.shape[-1], LANES)
    wts = {
        "gdn": {
            "norm_mix": norm_mix[0],
            "w_in": _pad_to(gdn_w_in, 2, _round_up(gdn_w_in.shape[2], LANES)),
            "conv_w": gdn_conv_w[0], "a_log": gdn_a_log[0], "dt_bias": gdn_dt_bias[0],
            "norm_w": gdn_norm_w[0], "w_out": bf(gdn_w_out),
        },
        "rwkv": {
            "norm_mix": norm_mix[1], "mix": rwkv_mix[0],
            "w_r": rwkv_w_r, "w_k": rwkv_w_k, "w_v": rwkv_w_v,
            "w0": rwkv_w0[0], "w1": bf(rwkv_w1[0]), "w2": bf(rwkv_w2[0]),
            "a0": rwkv_a0[0], "a1": bf(rwkv_a1[0]), "a2": bf(rwkv_a2[0]),
            "g1": bf(_pad_to(rwkv_g1[0], 1, gate_rank)), "g2": bf(_pad_to(rwkv_g2[0], 0, gate_rank)),
            "k_k": rwkv_k_k[0], "k_a": rwkv_k_a[0], "r_k": rwkv_r_k[0],
            "lnx_w": rwkv_lnx_w[0], "lnx_b": rwkv_lnx_b[0], "w_out": rwkv_w_out,
        },
        "ffn": {"norm": norm_ffn, "w_gate": ffn_w_gate, "w_up": ffn_w_up, "w_down": bf(ffn_w_down)},
        "norm_final": norm_final,
    }
    outs_p = _trunk(x_prompt, None, None, None, None, wts)
    outs_s = _trunk(x_sample, state_gdn_conv[0], state_gdn_ssm[0], state_rwkv_shift[0], state_rwkv_wkv[0], wts)
    return (outs_p[0], outs_s[0]) + tuple(outs_p[1:]) + tuple(outs_s[1:])
```

```python
import functools

import jax
import jax.numpy as jnp
from jax import lax
from jax.experimental import pallas as pl
from jax.experimental.pallas import tpu as pltpu

F32 = jnp.float32
BF16 = jnp.bfloat16
HIGHEST = lax.Precision.HIGHEST

D_MODEL = 4096
DEPTH = 2
GDN_HK = 32
GDN_HV = 64
GDN_DK = 128
GDN_DV = 128
GDN_CONV_W = 4
GDN_CHUNK = 64
RWKV_N = 64
RWKV_CHUNK = 64
RWKV_LNX_EPS = 64e-5
RMS_EPS = 1e-6
L2_EPS = 1e-6

LANES = 128
SUBLANES = 8
VMEM_LIMIT = 56 * 1024 * 1024


def _cparams(sem):
    return pltpu.CompilerParams(dimension_semantics=sem, vmem_limit_bytes=VMEM_LIMIT)


def _pick(n, prefs):
    for p in prefs:
        if n % p == 0:
            return p
    return n


def _dot(a, b, precision=None):
    return jnp.dot(a, b, preferred_element_type=F32, precision=precision)


def _sigmoid(x):
    return 1.0 / (1.0 + jnp.exp(-x))


def _softplus(x):
    return jnp.maximum(x, 0.0) + jnp.log1p(jnp.exp(-jnp.abs(x)))


def _tri_masks(c):
    ii = lax.broadcasted_iota(jnp.int32, (c, c), 0)
    jj = lax.broadcasted_iota(jnp.int32, (c, c), 1)
    return ii >= jj, ii > jj, ii == jj


def _bdot(a, b):
    return lax.dot_general(a, b, (((2,), (1,)), ((0,), (0,))), preferred_element_type=F32)


def _bdot_nt(a, b):
    return lax.dot_general(a, b, (((2,), (2,)), ((0,), (0,))), preferred_element_type=F32)


def _bdot_tn(a, b):
    return lax.dot_general(a, b, (((1,), (1,)), ((0,), (0,))), preferred_element_type=F32)


def _mxu(x, rows):
    return x.astype(BF16) if rows >= 2 * SUBLANES else x


def _chunk_cumsum(x, c):
    incl, _, _ = _tri_masks(c)
    tri = jnp.broadcast_to(jnp.where(incl, 1.0, 0.0).astype(BF16)[None], (x.shape[0], c, c))
    hi = x.astype(BF16)
    rest = x - hi.astype(F32)
    mid = rest.astype(BF16)
    lo = (rest - mid.astype(F32)).astype(BF16)
    return _bdot(tri, hi) + _bdot(tri, mid) + _bdot(tri, lo)


def _pair_masks(c):
    left = lax.broadcasted_iota(jnp.int32, (1, 1, 2 * c), 2) < c
    ii = lax.broadcasted_iota(jnp.int32, (c, 2 * c), 0)
    jj = lax.broadcasted_iota(jnp.int32, (c, 2 * c), 1)
    jm = jnp.where(jj < c, jj, jj - c)
    return left, ii >= jm, ii > jm, ii == jm


def _bd(x, left):
    return jnp.concatenate([jnp.where(left, x, 0.0), jnp.where(left, 0.0, x)], axis=1)


def _pair_inverse(a, c):
    left, _, _, diag = _pair_masks(c)
    x = jnp.where(diag, 1.0, 0.0).astype(F32)[None] + a
    pw = _bdot(_mxu(a, c), _mxu(_bd(a, left), c))
    power = 2
    while power < c:
        if 2 * power < c:
            y = _bdot(_mxu(pw, c), _mxu(jnp.concatenate([_bd(x, left), _bd(pw, left)], axis=2), c))
            x = x + y[:, :, :2 * c]
            pw = y[:, :, 2 * c:]
        else:
            x = x + _bdot(_mxu(pw, c), _mxu(_bd(x, left), c))
        power *= 2
    return x


def _take(x, idx):
    return jnp.stack([x[i] for i in idx])


def _mm_kernel(*refs, nk, has_res):
    if has_res:
        a_ref, b_ref, r_ref, o_ref = refs[:4]
        scratch = refs[4:]
    else:
        a_ref, b_ref, o_ref = refs[:3]
        r_ref = None
        scratch = refs[3:]
    part = _dot(a_ref[...], b_ref[...])

    def finish(total):
        if has_res:
            total = total + r_ref[...]
        o_ref[...] = total.astype(o_ref.dtype)

    if nk == 1:
        finish(part)
        return
    acc_ref = scratch[0]
    k = pl.program_id(2)

    @pl.when(k == 0)
    def _():
        acc_ref[...] = part

    @pl.when(jnp.logical_and(k > 0, k < nk - 1))
    def _():
        acc_ref[...] += part

    @pl.when(k == nk - 1)
    def _():
        finish(acc_ref[...] + part)


def matmul(a, b, layer, res=None, out_dtype=F32, bm=None, bn=None, bk=None):
    m, kdim = a.shape
    n = b.shape[2]
    bm = bm or _pick(m, (1024, 512, 256, 128, 64, 32, 16, 8))
    bk = bk or kdim
    bn = bn or _pick(n, (1024, 512, 256, 128) if (res is None and bk == kdim) else (512, 256, 128))
    nk = kdim // bk
    in_specs = [pl.BlockSpec((bm, bk), lambda i, j, k: (i, k)),
                pl.BlockSpec((None, bk, bn), lambda i, j, k: (layer, k, j))]
    args = [a, b]
    if res is not None:
        in_specs.append(pl.BlockSpec((bm, bn), lambda i, j, k: (i, j)))
        args.append(res)
    scratch = [pltpu.VMEM((bm, bn), F32)] if nk > 1 else []
    return pl.pallas_call(
        functools.partial(_mm_kernel, nk=nk, has_res=res is not None),
        grid=(m // bm, n // bn, nk),
        in_specs=in_specs,
        out_specs=pl.BlockSpec((bm, bn), lambda i, j, k: (i, j)),
        out_shape=jax.ShapeDtypeStruct((m, n), out_dtype),
        scratch_shapes=scratch,
        compiler_params=_cparams(("parallel", "parallel", "arbitrary")),
        name="matmul",
    )(*args)


def _mm_ws_kernel(*refs, has_res):
    if has_res:
        a_ref, w_ref, r_ref, o_ref, wb_ref = refs
    else:
        a_ref, w_ref, o_ref, wb_ref = refs
        r_ref = None

    @pl.when(pl.program_id(1) == 0)
    def _():
        wb_ref[...] = w_ref[...].astype(BF16)

    total = _dot(a_ref[...], wb_ref[...])
    if has_res:
        total = total + r_ref[...]
    o_ref[...] = total.astype(o_ref.dtype)


def matmul_ws(a, w, layer, n_out=None, col0=0, res=None, out_dtype=F32, bn=None):
    m, kdim = a.shape
    n_out = n_out or w.shape[2]
    bm = _pick(m, (1024, 512, 256, 128, 64, 32, 16, 8))
    bn = bn or _pick(n_out, (512, 256, 128))
    assert col0 % bn == 0 and n_out % bn == 0
    off = col0 // bn
    in_specs = [pl.BlockSpec((bm, kdim), lambda j, i: (i, 0)),
                pl.BlockSpec((None, kdim, bn), lambda j, i: (layer, 0, off + j))]
    args = [a, w]
    if res is not None:
        in_specs.append(pl.BlockSpec((bm, bn), lambda j, i: (i, j)))
        args.append(res)
    return pl.pallas_call(
        functools.partial(_mm_ws_kernel, has_res=res is not None),
        grid=(n_out // bn, m // bm),
        in_specs=in_specs,
        out_specs=pl.BlockSpec((bm, bn), lambda j, i: (i, j)),
        out_shape=jax.ShapeDtypeStruct((m, n_out), out_dtype),
        scratch_shapes=[pltpu.VMEM((kdim, bn), BF16)],
        compiler_params=_cparams(("arbitrary", "arbitrary")),
        name="matmul_ws",
    )(*args)


def _gate_up_kernel(h_ref, wg_ref, wu_ref, o_ref, wgb_ref, wub_ref):
    @pl.when(pl.program_id(1) == 0)
    def _():
        wgb_ref[...] = wg_ref[...].astype(BF16)
        wub_ref[...] = wu_ref[...].astype(BF16)

    h = h_ref[...]
    g = _dot(h, wgb_ref[...])
    u = _dot(h, wub_ref[...])
    o_ref[...] = (g * _sigmoid(g) * u).astype(o_ref.dtype)


def gate_up(h, wg, wu, layer):
    m, kdim = h.shape
    n = wg.shape[2]
    bm = _pick(m, (1024, 512, 256, 128, 64, 32, 16, 8))
    bn = _pick(n, (256, 128))
    wspec = pl.BlockSpec((None, kdim, bn), lambda j, i: (layer, 0, j))
    return pl.pallas_call(
        _gate_up_kernel,
        grid=(n // bn, m // bm),
        in_specs=[pl.BlockSpec((bm, kdim), lambda j, i: (i, 0)), wspec, wspec],
        out_specs=pl.BlockSpec((bm, bn), lambda j, i: (i, j)),
        out_shape=jax.ShapeDtypeStruct((m, n), BF16),
        scratch_shapes=[pltpu.VMEM((kdim, bn), BF16), pltpu.VMEM((kdim, bn), BF16)],
        compiler_params=_cparams(("arbitrary", "arbitrary")),
        name="ffn_gate_up",
    )(h, wg, wu)


def _rms_kernel(x_ref, w_ref, o_ref):
    x = x_ref[...]
    y = x * lax.rsqrt(jnp.mean(x * x, axis=-1, keepdims=True) + RMS_EPS)
    o_ref[...] = (y * w_ref[...]).astype(o_ref.dtype)


def rmsnorm(x, w, out_dtype):
    m, d = x.shape
    br = _pick(m, (256, 128, 64, 32, 16, 8))
    return pl.pallas_call(
        _rms_kernel,
        grid=(m // br,),
        in_specs=[pl.BlockSpec((br, d), lambda i: (i, 0)),
                  pl.BlockSpec((1, d), lambda i: (0, 0))],
        out_specs=pl.BlockSpec((br, d), lambda i: (i, 0)),
        out_shape=jax.ShapeDtypeStruct((m, d), out_dtype),
        compiler_params=_cparams(("parallel",)),
        name="rmsnorm",
    )(x, w.reshape(1, d))


def _gdn_prep_kernel(x_ref, buf_ref, cw_ref, o_ref, win_ref, *, lb, n_q_blocks, n_qk_blocks, q_scale):
    cblk = pl.program_id(1)
    l = pl.program_id(2)
    halo = GDN_CONV_W - 1
    top = SUBLANES

    @pl.when(l == 0)
    def _():
        win_ref[:, top - halo:top, :] = buf_ref[...]

    win_ref[:, top:top + lb, :] = x_ref[...]
    acc = win_ref[:, top - halo:top - halo + lb, :] * cw_ref[0:1, :]
    for j in range(1, GDN_CONV_W):
        acc = acc + win_ref[:, top - halo + j:top - halo + j + lb, :] * cw_ref[j:j + 1, :]
    carry = win_ref[:, top + lb - halo:top + lb, :]
    win_ref[:, top - halo:top, :] = carry
    y = acc * _sigmoid(acc)

    @pl.when(cblk >= n_qk_blocks)
    def _():
        o_ref[...] = y

    @pl.when(cblk < n_qk_blocks)
    def _():
        scale = jnp.where(cblk < n_q_blocks, q_scale, 1.0).astype(F32)
        for s in range(y.shape[-1] // GDN_DK):
            ys = y[:, :, s * GDN_DK:(s + 1) * GDN_DK]
            inv = lax.rsqrt(jnp.sum(ys * ys, axis=-1, keepdims=True) + L2_EPS) * scale
            o_ref[:, :, s * GDN_DK:(s + 1) * GDN_DK] = ys * inv


def gdn_prep(proj, conv_buf, conv_w):
    b, l, _ = proj.shape
    kdim = GDN_HK * GDN_DK
    conv_dim = conv_w.shape[1]
    cb = _pick(kdim, (2048, 1024, 512, 256, 128))
    lb = _pick(l, (256, 128, 64, 32, 16, 8))
    bb = _pick(b, (16, 8, 4, 2, 1)) if l < SUBLANES else 1
    kern = functools.partial(_gdn_prep_kernel, lb=lb, n_q_blocks=kdim // cb, n_qk_blocks=2 * kdim // cb,
                             q_scale=float(GDN_DK) ** -0.5)
    return pl.pallas_call(
        kern,
        grid=(b // bb, conv_dim // cb, l // lb),
        in_specs=[pl.BlockSpec((bb, lb, cb), lambda i, c, t: (i, t, c)),
                  pl.BlockSpec((bb, GDN_CONV_W - 1, cb), lambda i, c, t: (i, 0, c)),
                  pl.BlockSpec((GDN_CONV_W, cb), lambda i, c, t: (0, c))],
        out_specs=pl.BlockSpec((bb, lb, cb), lambda i, c, t: (i, t, c)),
        out_shape=jax.ShapeDtypeStruct((b, l, conv_dim), F32),
        scratch_shapes=[pltpu.VMEM((bb, SUBLANES + lb, cb), F32)],
        compiler_params=_cparams(("parallel", "parallel", "arbitrary")),
        name="gdn_prep",
    )(proj, conv_buf, conv_w)


def _in_proj_conv_kernel(a_ref, w_ref, buf_ref, cw_ref, o_ref, st_ref, wb_ref, win_ref, *,
                         bm, bn, n_blocks, blocks_per_seq, n_q_blocks, n_qk_blocks, q_scale):
    j = pl.program_id(0)
    i = pl.program_id(1)
    halo = GDN_CONV_W - 1
    top = SUBLANES
    slot = i % 2

    @pl.when(i == 0)
    def _():
        wb_ref[...] = w_ref[...].astype(BF16)

    def finish_previous_block():
        prev = 1 - slot
        scale = jnp.where(j < n_q_blocks, q_scale, 1.0).astype(F32)
        normed = j < n_qk_blocks
        acc = win_ref[prev, top - halo:top - halo + bm, :] * cw_ref[0:1, :]
        for tap in range(1, GDN_CONV_W):
            acc = acc + win_ref[prev, top - halo + tap:top - halo + tap + bm, :] * cw_ref[tap:tap + 1, :]
        y = acc * _sigmoid(acc)
        for s0 in range(0, bn, GDN_DK):
            ys = y[:, s0:s0 + GDN_DK]
            inv = lax.rsqrt(jnp.sum(ys * ys, axis=-1, keepdims=True) + L2_EPS) * scale
            o_ref[:, s0:s0 + GDN_DK] = ys * jnp.where(normed, inv, 1.0)
        st_ref[0] = win_ref[prev, top + bm - halo:top + bm, :]

    def multiply_this_block():
        win_ref[slot, top:top + bm, :] = _dot(a_ref[...], wb_ref[...])

        @pl.when(i % blocks_per_seq == 0)
        def _():
            win_ref[slot, top - halo:top, :] = buf_ref[0]

        @pl.when(i % blocks_per_seq != 0)
        def _():
            win_ref[slot, top - halo:top, :] = win_ref[1 - slot, top + bm - halo:top + bm, :]

    @pl.when(i == 0)
    def _():
        multiply_this_block()

    @pl.when(jnp.logical_and(i > 0, i < n_blocks))
    def _():
        finish_previous_block()
        multiply_this_block()

    @pl.when(i == n_blocks)
    def _():
        finish_previous_block()


def gdn_in_proj_conv(h, w_in, conv_buf, conv_w, b, l):
    m, kdim = h.shape
    conv_dim = conv_w.shape[1]
    key_dim = GDN_HK * GDN_DK
    bm = _pick(l, (1024, 512, 256, 128, 64))
    bn = _pick(key_dim, (512, 256))
    bps = l // bm
    nblk = m // bm
    kern = functools.partial(_in_proj_conv_kernel, bm=bm, bn=bn, n_blocks=nblk, blocks_per_seq=bps,
                             n_q_blocks=key_dim // bn, n_qk_blocks=2 * key_dim // bn, q_scale=float(GDN_DK) ** -0.5)
    this_blk = lambda i: jnp.minimum(i, nblk - 1)
    prev_blk = lambda i: jnp.maximum(i - 1, 0)
    return pl.pallas_call(
        kern,
        grid=(conv_dim // bn, nblk + 1),
        in_specs=[pl.BlockSpec((bm, kdim), lambda j, i: (this_blk(i), 0)),
                  pl.BlockSpec((None, kdim, bn), lambda j, i: (0, 0, j)),
                  pl.BlockSpec((1, GDN_CONV_W - 1, bn), lambda j, i: (this_blk(i) // bps, 0, j)),
                  pl.BlockSpec((GDN_CONV_W, bn), lambda j, i: (0, j))],
        out_specs=[pl.BlockSpec((bm, bn), lambda j, i: (prev_blk(i), j)),
                   pl.BlockSpec((1, GDN_CONV_W - 1, bn), lambda j, i: (prev_blk(i) // bps, 0, j))],
        out_shape=[jax.ShapeDtypeStruct((m, conv_dim), F32),
                   jax.ShapeDtypeStruct((b, GDN_CONV_W - 1, conv_dim), F32)],
        scratch_shapes=[pltpu.VMEM((kdim, bn), BF16),
                        pltpu.VMEM((2, SUBLANES + bm, bn), F32)],
        compiler_params=_cparams(("arbitrary", "arbitrary")),
        name="gdn_in_proj_conv",
    )(h, w_in, conv_buf, conv_w)


def _gdn_gates_kernel(ba_ref, alog_ref, dtb_ref, beta_ref, gc_ref, pad_ref, *, lb, lpb, chunk):
    hv = GDN_HV
    ba = ba_ref[0]
    beta = _sigmoid(ba[:, :hv])
    g = -jnp.exp(alog_ref[...]) * _softplus(ba[:, hv:2 * hv] + dtb_ref[...])
    if lpb != lb:
        pad_ref[...] = jnp.zeros_like(pad_ref)
        pad_ref[0:lb, :] = g
        g = pad_ref[...]
        pad_ref[0:lb, :] = beta
        beta = pad_ref[...]
    ii = lax.broadcasted_iota(jnp.int32, (lpb, lpb), 0)
    jj = lax.broadcasted_iota(jnp.int32, (lpb, lpb), 1)
    same_chunk_causal = jnp.logical_and(ii >= jj, ii // chunk == jj // chunk)
    gc = _dot(jnp.where(same_chunk_causal, 1.0, 0.0).astype(F32), g, HIGHEST)
    beta_ref[0] = beta
    gc_ref[0] = gc


def gdn_gates(ba, a_log, dt_bias, chunk, lp):
    b, l, w = ba.shape
    lb = _pick(l, (256, 128, 64, 32, 16, 8))
    lpb = lb if lp == l else lp
    kern = functools.partial(_gdn_gates_kernel, lb=lb, lpb=lpb, chunk=chunk)
    out = jax.ShapeDtypeStruct((b, lp, GDN_HV), F32)
    return pl.pallas_call(
        kern,
        grid=(b, l // lb),
        in_specs=[pl.BlockSpec((1, lb, w), lambda i, t: (i, t, 0)),
                  pl.BlockSpec((1, GDN_HV), lambda i, t: (0, 0)),
                  pl.BlockSpec((1, GDN_HV), lambda i, t: (0, 0))],
        out_specs=[pl.BlockSpec((1, lpb, GDN_HV), lambda i, t: (i, t, 0)),
                   pl.BlockSpec((1, lpb, GDN_HV), lambda i, t: (i, t, 0))],
        out_shape=[out, out],
        scratch_shapes=[pltpu.VMEM((lpb, GDN_HV), F32)],
        compiler_params=_cparams(("parallel", "parallel")),
        name="gdn_gates",
    )(ba, a_log.reshape(1, GDN_HV), dt_bias.reshape(1, GDN_HV))


def _gdn_delta_kernel(*refs, l_real, lb, chunk, hb, has_s0):
    if has_s0:
        q_ref, k_ref, v_ref, z_ref, beta_ref, gc_ref, gct_ref, nw_ref, s0_ref = refs[:9]
        rest = refs[9:]
    else:
        q_ref, k_ref, v_ref, z_ref, beta_ref, gc_ref, gct_ref, nw_ref = refs[:8]
        s0_ref = None
        rest = refs[8:]
    o_ref, so_ref, s_scr, padk_scr, padv_scr = rest
    hblk = pl.program_id(1)
    t = pl.program_id(2)
    nt = pl.num_programs(2)
    c = chunk
    nc = lb // c
    padded = l_real < lb
    rep = GDN_HV // GDN_HK
    kh = hb // rep
    pairs = hb // 2
    left_c, incl, strict, _ = _pair_masks(c)

    @pl.when(t == 0)
    def _():
        if has_s0:
            s_scr[...] = s0_ref[0]
        else:
            s_scr[...] = jnp.zeros_like(s_scr)

    def load(ref, pad_scr):
        if padded:
            pad_scr[...] = jnp.zeros_like(pad_scr)
            pad_scr[0:l_real, :] = ref[0]
            return pad_scr[...]
        return ref[0]

    q2 = load(q_ref, padk_scr)
    k2 = load(k_ref, padk_scr)
    v2 = load(v_ref, padv_scr)
    beta_t = beta_ref[0]
    gc_t = gc_ref[0]
    lane = lax.broadcasted_iota(jnp.int32, (lb, GDN_HV), 1)

    q3, k3, qk2, kk2 = [], [], [], []
    for p in range(kh):
        qp = q2[:, p * GDN_DK:(p + 1) * GDN_DK].reshape(nc, c, GDN_DK)
        kp = k2[:, p * GDN_DK:(p + 1) * GDN_DK].reshape(nc, c, GDN_DK)
        q3.append(qp)
        k3.append(kp)
        kp_twice = _mxu(jnp.concatenate([kp, kp], axis=1), c)
        qk2.append(_bdot_nt(_mxu(qp, c), kp_twice))
        kk2.append(_bdot_nt(_mxu(kp, c), kp_twice))

    def column(tile, head):
        return jnp.sum(jnp.where(lane == head, tile, 0.0), axis=-1, keepdims=True).reshape(nc, c, 1)

    zero = jnp.zeros((nc, c, GDN_DV + GDN_DK), F32)
    m_l, rhs_l, ai_l = [], [], []
    wq_l, kg_l, gl_l = [], [], []
    for pp in range(pairs):
        kp, qp = k3[2 * pp // rep], q3[2 * pp // rep]
        beta = [column(beta_t, hblk * hb + 2 * pp + e) for e in range(2)]
        gcol = [column(gc_t, hblk * hb + 2 * pp + e) for e in range(2)]
        grow = gct_ref[0, pp]
        diff = jnp.where(left_c, gcol[0], gcol[1]) - grow
        decay = jnp.where(incl, jnp.exp(jnp.where(incl, diff, 0.0)), 0.0)
        ai_l.append(_mxu(qk2[2 * pp // rep] * decay, c))
        m_l.append(jnp.where(strict, kk2[2 * pp // rep] * (jnp.where(left_c, beta[0], beta[1]) * decay), 0.0))
        blocks = []
        for e in range(2):
            j = 2 * pp + e
            eg = jnp.exp(gcol[e])
            g_last = gcol[e][:, c - 1:c, :]
            vj = v2[:, j * GDN_DV:(j + 1) * GDN_DV].reshape(nc, c, GDN_DV)
            blocks.append(jnp.concatenate([vj * beta[e], kp * (beta[e] * eg)], axis=-1))
            wq_l.append(qp * eg)
            kg_l.append(_mxu(kp * jnp.exp(g_last - gcol[e]), c))
            gl_l.append(jnp.exp(g_last))
        rhs_l.append(jnp.concatenate([jnp.concatenate([blocks[0], zero], axis=-1),
                                      jnp.concatenate([zero, blocks[1]], axis=-1)], axis=1))
    cat = lambda xs: jnp.concatenate(xs, axis=0)
    tinv = _mxu(_pair_inverse(-cat(m_l), c), c)
    uw = _bdot(tinv, _mxu(cat(rhs_l), c))
    a_intra = cat(ai_l)
    width = GDN_DV + GDN_DK
    u_l, wq_rows = [], []
    for pp in range(pairs):
        for e in range(2):
            part = uw[pp * nc:(pp + 1) * nc, :, e * width:(e + 1) * width]
            u_l.append(part[:, :, :GDN_DV])
            wq_rows.append(_mxu(jnp.concatenate([part[:, :, GDN_DV:], wq_l[2 * pp + e]], axis=1), c))

    s = s_scr[...]
    zero_v = jnp.zeros((pairs, c, GDN_DV), F32)
    for ci in range(nc):
        pick = lambda xs: jnp.stack([x[ci] for x in xs])
        ws_qs = _bdot(pick(wq_rows), _mxu(s, c))
        v_new = pick(u_l) - ws_qs[:, :c]
        v_newb = _mxu(v_new, c)
        v_even = jnp.stack([v_new[2 * pp] for pp in range(pairs)])
        v_odd = jnp.stack([v_new[2 * pp + 1] for pp in range(pairs)])
        v_bd = jnp.concatenate([jnp.concatenate([v_even, zero_v], axis=-1),
                                jnp.concatenate([zero_v, v_odd], axis=-1)], axis=1)
        av = _bdot(_take(a_intra, [pp * nc + ci for pp in range(pairs)]), _mxu(v_bd, c))
        o = ws_qs[:, c:] + jnp.stack([av[j // 2][:, (j % 2) * GDN_DV:(j % 2 + 1) * GDN_DV] for j in range(hb)])
        s = s * pick(gl_l) + _bdot_tn(pick(kg_l), v_newb)
        o = o * lax.rsqrt(jnp.mean(o * o, axis=-1, keepdims=True) + RMS_EPS) * nw_ref[...]
        for j in range(hb):
            vslab = slice(j * GDN_DV, (j + 1) * GDN_DV)
            if padded:
                z = z_ref[0, :, vslab]
                o_ref[0, :, vslab] = (o[j][0:l_real] * (z * _sigmoid(z))).astype(o_ref.dtype)
            else:
                z = z_ref[0, ci * c:(ci + 1) * c, vslab]
                o_ref[0, ci * c:(ci + 1) * c, vslab] = (o[j] * (z * _sigmoid(z))).astype(o_ref.dtype)
    s_scr[...] = s

    @pl.when(t == nt - 1)
    def _():
        so_ref[0] = s


def gdn_delta(qkv, zsrc, z_col0, beta, gc, gct, norm_w, s0, l_real, chunk, lb, hb):
    b = qkv.shape[0]
    lp = beta.shape[1]
    lrows = l_real if l_real < lb else lb
    nt = lp // lb
    rep = GDN_HV // GDN_HK
    kh = hb // rep
    k_base = GDN_HK // kh
    v_base = (2 * GDN_HK * GDN_DK) // (hb * GDN_DV)
    z_base = z_col0 // (hb * GDN_DV)
    in_specs = [
        pl.BlockSpec((1, lrows, kh * GDN_DK), lambda i, h, t: (i, t, h)),
        pl.BlockSpec((1, lrows, kh * GDN_DK), lambda i, h, t: (i, t, k_base + h)),
        pl.BlockSpec((1, lrows, hb * GDN_DV), lambda i, h, t: (i, t, v_base + h)),
        pl.BlockSpec((1, lrows, hb * GDN_DV), lambda i, h, t: (i, t, z_base + h)),
        pl.BlockSpec((1, lb, GDN_HV), lambda i, h, t: (i, t, 0)),
        pl.BlockSpec((1, lb, GDN_HV), lambda i, h, t: (i, t, 0)),
        pl.BlockSpec((1, hb // 2, lb // chunk, 1, 2 * chunk), lambda i, h, t: (i, h, t, 0, 0)),
        pl.BlockSpec((1, GDN_DV), lambda i, h, t: (0, 0)),
    ]
    args = [qkv, qkv, qkv, zsrc, beta, gc, gct, norm_w.reshape(1, GDN_DV)]
    if s0 is not None:
        in_specs.append(pl.BlockSpec((1, hb, GDN_DK, GDN_DV), lambda i, h, t: (i, h, 0, 0)))
        args.append(s0)
    kern = functools.partial(_gdn_delta_kernel, l_real=l_real, lb=lb, chunk=chunk, hb=hb, has_s0=s0 is not None)
    return pl.pallas_call(
        kern,
        grid=(b, GDN_HV // hb, nt),
        in_specs=in_specs,
        out_specs=[pl.BlockSpec((1, lrows, hb * GDN_DV), lambda i, h, t: (i, t, h)),
                   pl.BlockSpec((1, hb, GDN_DK, GDN_DV), lambda i, h, t: (i, h, 0, 0))],
        out_shape=[jax.ShapeDtypeStruct((b, l_real, GDN_HV * GDN_DV), BF16),
                   jax.ShapeDtypeStruct((b, GDN_HV, GDN_DK, GDN_DV), F32)],
        scratch_shapes=[pltpu.VMEM((hb, GDN_DK, GDN_DV), F32),
                        pltpu.VMEM((lb, kh * GDN_DK), F32),
                        pltpu.VMEM((lb, hb * GDN_DV), F32)],
        compiler_params=_cparams(("parallel", "parallel", "arbitrary")),
        name="gdn_delta",
    )(*args)


def _rwkv_mix_kernel(x_ref, nw_ref, shift_ref, mix_ref, o0, o1, o2, o3, o4, o5, last_ref, win_ref, *, lb):
    t = pl.program_id(1)
    top = SUBLANES

    @pl.when(t == 0)
    def _():
        win_ref[top - 1:top, :] = shift_ref[0]

    x = x_ref[0]
    h = x * lax.rsqrt(jnp.mean(x * x, axis=-1, keepdims=True) + RMS_EPS) * nw_ref[...]
    win_ref[top:top + lb, :] = h
    prev = win_ref[top - 1:top - 1 + lb, :]
    last = win_ref[top + lb - 1:top + lb, :]
    win_ref[top - 1:top, :] = last
    last_ref[0] = last
    xx = prev - h
    for i, o in enumerate((o0, o1, o2, o3, o4, o5)):
        o[0] = (h + xx * mix_ref[i:i + 1, :]).astype(o.dtype)


def rwkv_mix(x, norm_w, shift, mix):
    b, l, d = x.shape
    lb = _pick(l, (128, 64, 32, 16, 8))
    kern = functools.partial(_rwkv_mix_kernel, lb=lb)
    tok = pl.BlockSpec((1, lb, d), lambda i, t: (i, t, 0))
    row = pl.BlockSpec((1, 1, d), lambda i, t: (i, 0, 0))
    outs = pl.pallas_call(
        kern,
        grid=(b, l // lb),
        in_specs=[tok, pl.BlockSpec((1, d), lambda i, t: (0, 0)), row,
                  pl.BlockSpec((6, d), lambda i, t: (0, 0))],
        out_specs=[tok] * 6 + [row],
        out_shape=[jax.ShapeDtypeStruct((b, l, d), BF16)] * 6 + [jax.ShapeDtypeStruct((b, 1, d), F32)],
        scratch_shapes=[pltpu.VMEM((SUBLANES + lb, d), F32)],
        compiler_params=_cparams(("parallel", "arbitrary")),
        name="rwkv_mix",
    )(x, norm_w.reshape(1, d), shift.reshape(b, 1, d), mix)
    return outs[:6], outs[6].reshape(b, d)


def _lora_kernel(x_ref, w1_ref, o_ref, *, act):
    hmid = _dot(x_ref[...], w1_ref[...])
    if act == "tanh":
        hmid = jnp.tanh(hmid)
    elif act == "sigmoid":
        hmid = _sigmoid(hmid)
    o_ref[...] = hmid


def lora_mid(x, w1, act):
    m, d = x.shape
    r = w1.shape[1]
    bm = _pick(m, (512, 256, 128, 64, 32, 16, 8))
    return pl.pallas_call(
        functools.partial(_lora_kernel, act=act),
        grid=(m // bm,),
        in_specs=[pl.BlockSpec((bm, d), lambda i: (i, 0)),
                  pl.BlockSpec((d, r), lambda i: (0, 0))],
        out_specs=pl.BlockSpec((bm, r), lambda i: (i, 0)),
        out_shape=jax.ShapeDtypeStruct((m, r), F32),
        compiler_params=_cparams(("parallel",)),
        name="rwkv_lora",
    )(x, w1)


def _rwkv_pair_kernel(*refs, l_real, lb, chunk, has_s0):
    n_in = 17 if has_s0 else 16
    (r_ref, k_ref, v_ref, hw_ref, ha_ref, hg_ref, w2_ref, a2_ref, g2_ref, w0_ref, a0_ref,
     kk_ref, ka_ref, rk_ref, lnw_ref, lnb_ref) = refs[:16]
    s0_ref = refs[16] if has_s0 else None
    o_ref, so_ref, s_scr, pad_scr, padm_scr = refs[n_in:]
    t = pl.program_id(2)
    nt = pl.num_programs(2)
    n = RWKV_N
    c = chunk
    nc = lb // c
    width = r_ref.shape[-1]
    pairs = width // (2 * n)
    padded = l_real < lb
    left_n = lax.broadcasted_iota(jnp.int32, (1, 1, 2 * n), 2) < n
    _, incl, strict, _ = _pair_masks(c)
    rr = lax.broadcasted_iota(jnp.int32, (2 * n, 2 * n), 0) < n
    cc = lax.broadcasted_iota(jnp.int32, (2 * n, 2 * n), 1) < n
    same_head = rr == cc
    bd = _bd

    def head_sum(x):
        sl = jnp.sum(jnp.where(left_n, x, 0.0), axis=-1, keepdims=True)
        sr = jnp.sum(jnp.where(left_n, 0.0, x), axis=-1, keepdims=True)
        return jnp.where(left_n, sl, sr)

    @pl.when(t == 0)
    def _():
        for p in range(pairs):
            if has_s0:
                zero = jnp.zeros((n, n), F32)
                top = jnp.concatenate([s0_ref[0, 2 * p], zero], axis=1)
                bot = jnp.concatenate([zero, s0_ref[0, 2 * p + 1]], axis=1)
                s_scr[p] = jnp.concatenate([top, bot], axis=0)
            else:
                s_scr[p] = jnp.zeros((2 * n, 2 * n), F32)

    def load(ref):
        if padded:
            pad_scr[...] = jnp.zeros_like(pad_scr)
            pad_scr[0:l_real, :] = ref[0]
            return pad_scr[...]
        return ref[0]

    def second_factor(h_ref, f_ref):
        rank = h_ref.shape[-1]
        if padded:
            padm_scr[...] = jnp.zeros_like(padm_scr)
            padm_scr[0:l_real, 0:rank] = h_ref[0]
            hmid = padm_scr[:, 0:rank]
        else:
            hmid = h_ref[0]
        return _dot(hmid.astype(BF16), f_ref[...])

    r2, k2, v2 = (load(ref) for ref in (r_ref, k_ref, v_ref))
    w2 = w0_ref[...] + second_factor(hw_ref, w2_ref)
    a2 = a0_ref[...] + second_factor(ha_ref, a2_ref)
    g2 = second_factor(hg_ref, g2_ref)
    lw2 = -jnp.exp(-_softplus(-w2) - 0.5)
    if padded:
        rows = lax.broadcasted_iota(jnp.int32, lw2.shape, 0)
        lw2 = jnp.where(rows < l_real, lw2, 0.0)
    a_sig = _sigmoid(a2)
    kmod = k2 * (1.0 + (a_sig - 1.0) * ka_ref[...])
    cl2 = _chunk_cumsum(lw2.reshape(nc, c, width), c).reshape(lb, width)

    def split(x2):
        x3 = x2.reshape(nc, c, width)
        return jnp.concatenate([x3[:, :, p * 2 * n:(p + 1) * 2 * n] for p in range(pairs)], axis=0)

    def per_pair(p_ref):
        return jnp.concatenate([jnp.broadcast_to(p_ref[:, p * 2 * n:(p + 1) * 2 * n][None], (nc, 1, 2 * n))
                                for p in range(pairs)], axis=0)

    kk = split(k2 * kk_ref[...])
    kk = kk * lax.rsqrt(head_sum(kk * kk) + L2_EPS)
    ep = split(jnp.exp(cl2))
    em = split(jnp.exp(-cl2))
    a_h = -kk * split(jnp.exp(cl2 - lw2))
    b_h = kk * split(a_sig) * em
    r = split(r2)
    k = split(kmod)
    v = split(v2)
    r_h = r * ep
    k_h = k * em
    pc = ep[:, c - 1:c, :]

    amat = _bdot_nt(_mxu(jnp.concatenate([a_h, r_h], axis=1), c),
                    _mxu(jnp.concatenate([bd(k_h, left_n), bd(b_h, left_n)], axis=1), c))
    a_ak = jnp.where(strict, amat[:, :c, :2 * c], 0.0)
    a_ab = jnp.where(strict, amat[:, :c, 2 * c:], 0.0)
    a_rk = jnp.where(incl, amat[:, c:, :2 * c], 0.0)
    a_rb = jnp.where(incl, amat[:, c:, 2 * c:], 0.0)

    tinv = _mxu(_pair_inverse(a_ab, c), c)

    vbd = bd(v, left_n)
    akv = _bdot(_mxu(a_ak, c), _mxu(vbd, c))
    wu = _bdot(tinv, _mxu(jnp.concatenate([bd(a_h, left_n), bd(akv, left_n)], axis=2), c))
    wmat = wu[:, :, :2 * n]
    ut = wu[:, :, 2 * n:]
    top = jnp.concatenate([bd(wmat, left_n), bd(ut, left_n)], axis=2)
    bot = jnp.concatenate([jnp.zeros_like(vbd), vbd], axis=2)
    ry = _bdot(_mxu(jnp.concatenate([a_rb, a_rk], axis=2), c), _mxu(jnp.concatenate([top, bot], axis=1), c))
    rt = r_h + ry[:, :, :2 * n]
    yt = ry[:, :, 2 * n:]
    bt = b_h * pc
    btb = _mxu(bt, c)
    hmat = _bdot_tn(_mxu(jnp.concatenate([ut, v], axis=1), c),
                    _mxu(jnp.concatenate([bt, k_h * pc], axis=1), c))
    hmat = jnp.where(same_head, hmat, 0.0)
    rw = _mxu(jnp.concatenate([rt, wmat], axis=1), c)

    s = s_scr[...]
    ys = [None] * (pairs * nc)
    for ci in range(nc):
        idx = [p * nc + ci for p in range(pairs)]
        ys_sw = _bdot_nt(_take(rw, idx), _mxu(s, c))
        y_c = ys_sw[:, :c] + _take(yt, idx)
        swt = _mxu(ys_sw[:, c:], c)
        s = s * _take(pc, idx) + jnp.where(same_head, _bdot_tn(swt, _take(btb, idx)), 0.0) + _take(hmat, idx)
        for j, i in enumerate(idx):
            ys[i] = y_c[j]
    s_scr[...] = s
    y = jnp.stack(ys)

    yc = y - head_sum(y) * (1.0 / n)
    var = head_sum(yc * yc) * (1.0 / n)
    yn = yc * lax.rsqrt(var + RWKV_LNX_EPS) * per_pair(lnw_ref) + per_pair(lnb_ref)
    yn = yn + head_sum(r * k * per_pair(rk_ref)) * v
    out = (yn * split(g2)).astype(o_ref.dtype)
    for p in range(pairs):
        rows = out[p * nc:(p + 1) * nc].reshape(lb, 2 * n)
        if padded:
            o_ref[0, :, p * 2 * n:(p + 1) * 2 * n] = rows[0:l_real]
        else:
            o_ref[0, :, p * 2 * n:(p + 1) * 2 * n] = rows

    @pl.when(t == nt - 1)
    def _():
        for p in range(pairs):
            so_ref[0, 2 * p] = s[p][:n, :n]
            so_ref[0, 2 * p + 1] = s[p][n:, n:]


def rwkv_wkv(r, k, v, mids, seconds, w0, a0, k_k, k_a, r_k, lnx_w, lnx_b, s0, chunk, lb, heads):
    b, l, d = r.shape
    n = RWKV_N
    h = d // n
    width = heads * n
    lrows = l if l < lb else lb
    nt = max(1, l // lb)
    tok = pl.BlockSpec((1, lrows, width), lambda i, p, t: (i, t, p))
    par = pl.BlockSpec((1, width), lambda i, p, t: (0, p))
    st = pl.BlockSpec((1, heads, n, n), lambda i, p, t: (i, p, 0, 0))
    mid_specs = [pl.BlockSpec((1, lrows, m.shape[-1]), lambda i, p, t: (i, t, 0)) for m in mids]
    sec_specs = [pl.BlockSpec((s.shape[0], width), lambda i, p, t: (0, p)) for s in seconds]
    in_specs = [tok] * 3 + mid_specs + sec_specs + [par] * 7
    args = [r, k, v, *mids, *seconds, w0.reshape(1, d), a0.reshape(1, d), k_k.reshape(1, d), k_a.reshape(1, d),
            r_k.reshape(1, d), lnx_w.reshape(1, d), lnx_b.reshape(1, d)]
    if s0 is not None:
        in_specs.append(st)
        args.append(s0)
    kern = functools.partial(_rwkv_pair_kernel, l_real=l, lb=lb, chunk=chunk, has_s0=s0 is not None)
    return pl.pallas_call(
        kern,
        grid=(b, h // heads, nt),
        in_specs=in_specs,
        out_specs=[tok, st],
        out_shape=[jax.ShapeDtypeStruct((b, l, d), BF16),
                   jax.ShapeDtypeStruct((b, h, n, n), F32)],
        scratch_shapes=[pltpu.VMEM((heads // 2, 2 * n, 2 * n), F32), pltpu.VMEM((lb, width), F32),
                        pltpu.VMEM((lb, max(m.shape[-1] for m in mids)), F32)],
        compiler_params=_cparams(("parallel", "parallel", "arbitrary")),
        name="rwkv_wkv",
    )(*args)


def _round_up(x, m):
    return -(-x // m) * m


def _gdn_layer(x, conv_st, ssm_st, w):
    b, l, d = x.shape
    conv_dim = 2 * GDN_HK * GDN_DK + GDN_HV * GDN_DV
    x2 = x.reshape(b * l, d)
    h = rmsnorm(x2, w["norm_mix"], BF16)
    vdim = GDN_HV * GDN_DV
    main_cols = conv_dim + vdim
    ba_cols = _round_up(2 * GDN_HV, LANES)
    ba = matmul_ws(h, w["w_in"], 0, n_out=ba_cols, col0=main_cols, bn=LANES).reshape(b, l, -1)
    if conv_st is None:
        conv_st = jnp.zeros((b, GDN_CONV_W - 1, conv_dim), F32)
    if l >= GDN_CHUNK:
        qkv, new_conv = gdn_in_proj_conv(h, w["w_in"], conv_st, w["conv_w"], b, l)
        qkv = qkv.reshape(b, l, conv_dim)
        zsrc, z_col0 = matmul_ws(h, w["w_in"], 0, n_out=vdim, col0=conv_dim).reshape(b, l, vdim), 0
    else:
        proj = matmul_ws(h, w["w_in"], 0, n_out=main_cols).reshape(b, l, -1)
        new_conv = proj[:, l - (GDN_CONV_W - 1):, :conv_dim]
        qkv = gdn_prep(proj, conv_st, w["conv_w"])
        zsrc, z_col0 = proj, conv_dim
    chunk = GDN_CHUNK if l >= GDN_CHUNK else _round_up(l, SUBLANES)
    lp = _round_up(l, chunk)
    beta, gc = gdn_gates(ba, w["a_log"], w["dt_bias"], chunk, lp)
    gct = jnp.swapaxes(gc, 1, 2).reshape(b, GDN_HV // 2, 2, lp // chunk, chunk)
    gct = jnp.transpose(gct, (0, 1, 3, 2, 4)).reshape(b, GDN_HV // 2, lp // chunk, 1, 2 * chunk)
    if l >= GDN_CHUNK:
        lb, hb = _pick(lp, (512, 256, 128, 64)), _pick(GDN_HV, (8, 4, 2))
    else:
        lb, hb = lp, _pick(GDN_HV, (64, 32, 16, 8, 4, 2))
    o, s_new = gdn_delta(qkv, zsrc, z_col0, beta, gc, gct, w["norm_w"], ssm_st, l, chunk, lb, hb)
    x2 = matmul(o.reshape(b * l, -1), w["w_out"], 0, res=x2, bk=_pick(o.shape[-1], (4096,)))
    return x2.reshape(b, l, d), new_conv, s_new


def _rwkv_layer(x, shift_st, wkv_st, w):
    b, l, d = x.shape
    if shift_st is None:
        shift_st = jnp.zeros((b, d), F32)
    (xr, xw, xk, xv, xa, xg), new_shift = rwkv_mix(x, w["norm_mix"], shift_st, w["mix"])
    m = b * l
    flat = lambda t: t.reshape(m, d)
    r = matmul_ws(flat(xr), w["w_r"], 0)
    k = matmul_ws(flat(xk), w["w_k"], 0)
    v = matmul_ws(flat(xv), w["w_v"], 0)
    mids = [lora_mid(flat(xw), w["w1"], "tanh").reshape(b, l, -1),
            lora_mid(flat(xa), w["a1"], "none").reshape(b, l, -1),
            lora_mid(flat(xg), w["g1"], "sigmoid").reshape(b, l, -1)]
    chunk = RWKV_CHUNK if l >= RWKV_CHUNK else _round_up(l, SUBLANES)
    n_heads = d // RWKV_N
    if l >= RWKV_CHUNK:
        lb, heads = _pick(l, (512, 256, 128, 64)), _pick(n_heads, (8, 4, 2))
    else:
        lb, heads = chunk, _pick(n_heads, (64, 32, 16, 8, 4, 2))
    to3 = lambda t: t.reshape(b, l, d)
    y, s_new = rwkv_wkv(to3(r), to3(k), to3(v), mids, [w["w2"], w["a2"], w["g2"]], w["w0"], w["a0"],
                        w["k_k"], w["k_a"], w["r_k"], w["lnx_w"], w["lnx_b"], wkv_st, chunk, lb, heads)
    x2 = matmul_ws(y.reshape(m, d), w["w_out"], 0, res=x.reshape(m, d))
    return x2.reshape(b, l, d), new_shift, s_new


def _ffn(x, w, layer):
    b, l, d = x.shape
    x2 = x.reshape(b * l, d)
    h = rmsnorm(x2, w["norm"][layer], BF16)
    act = gate_up(h, w["w_gate"], w["w_up"], layer)
    dff = act.shape[1]
    bk = dff // 2 if (dff // 2) % LANES == 0 else dff
    x2 = matmul(act, w["w_down"], layer, res=x2, bn=_pick(d, (512, 256, 128)), bk=bk)
    return x2.reshape(b, l, d)


def _trunk(x, conv_st, ssm_st, shift_st, wkv_st, wts):
    x, conv_new, ssm_new = _gdn_layer(x, conv_st, ssm_st, wts["gdn"])
    x = _ffn(x, wts["ffn"], 0)
    x, shift_new, wkv_new = _rwkv_layer(x, shift_st, wkv_st, wts["rwkv"])
    x = _ffn(x, wts["ffn"], 1)
    b, l, d = x.shape
    y = rmsnorm(x.reshape(b * l, d), wts["norm_final"], F32).reshape(b, l, d)
    return y, conv_new[None], ssm_new[None], shift_new[None], wkv_new[None]


def _pad_to(x, axis, size):
    pad = size - x.shape[axis]
    if pad == 0:
        return x
    widths = [(0, 0)] * x.ndim
    widths[axis] = (0, pad)
    return jnp.pad(x, widths)


def kernel(x_prompt, x_sample, state_gdn_conv, state_gdn_ssm, state_rwkv_shift, state_rwkv_wkv,
           norm_mix, norm_ffn, norm_final,
           gdn_w_in, gdn_conv_w, gdn_a_log, gdn_dt_bias, gdn_norm_w, gdn_w_out,
           rwkv_mix, rwkv_w_r, rwkv_w_k, rwkv_w_v, rwkv_w0, rwkv_w1, rwkv_w2,
           rwkv_a0, rwkv_a1, rwkv_a2, rwkv_g1, rwkv_g2, rwkv_k_k, rwkv_k_a, rwkv_r_k,
           rwkv_lnx_w, rwkv_lnx_b, rwkv_w_out,
           ffn_w_gate, ffn_w_up, ffn_w_down):
    bf = lambda t: t.astype(BF16)
    gate_rank = _round_up(rwkv_g1.shape[-1], LANES)
    wts = {
        "gdn": {
            "norm_mix": norm_mix[0],
            "w_in": _pad_to(gdn_w_in, 2, _round_up(gdn_w_in.shape[2], LANES)),
            "conv_w": gdn_conv_w[0], "a_log": gdn_a_log[0], "dt_bias": gdn_dt_bias[0],
            "norm_w": gdn_norm_w[0], "w_out": bf(gdn_w_out),
        },
        "rwkv": {
            "norm_mix": norm_mix[1], "mix": rwkv_mix[0],
            "w_r": rwkv_w_r, "w_k": rwkv_w_k, "w_v": rwkv_w_v,
            "w0": rwkv_w0[0], "w1": bf(rwkv_w1[0]), "w2": bf(rwkv_w2[0]),
            "a0": rwkv_a0[0], "a1": bf(rwkv_a1[0]), "a2": bf(rwkv_a2[0]),
            "g1": bf(_pad_to(rwkv_g1[0], 1, gate_rank)), "g2": bf(_pad_to(rwkv_g2[0], 0, gate_rank)),
            "k_k": rwkv_k_k[0], "k_a": rwkv_k_a[0], "r_k": rwkv_r_k[0],
            "lnx_w": rwkv_lnx_w[0], "lnx_b": rwkv_lnx_b[0], "w_out": rwkv_w_out,
        },
        "ffn": {"norm": norm_ffn, "w_gate": ffn_w_gate, "w_up": ffn_w_up, "w_down": bf(ffn_w_down)},
        "norm_final": norm_final,
    }
    outs_p = _trunk(x_prompt, None, None, None, None, wts)
    outs_s = _trunk(x_sample, state_gdn_conv[0], state_gdn_ssm[0], state_rwkv_shift[0], state_rwkv_wkv[0], wts)
    return (outs_p[0], outs_s[0]) + tuple(outs_p[1:]) + tuple(outs_s[1:])
```

```python
import functools

import jax
import jax.numpy as jnp
from jax import lax
from jax.experimental import pallas as pl
from jax.experimental.pallas import tpu as pltpu

F32 = jnp.float32
BF16 = jnp.bfloat16
HIGHEST = lax.Precision.HIGHEST

GDN_HK = 32
GDN_HV = 64
GDN_DK = 128
GDN_DV = 128
GDN_CONV_W = 4
GDN_CHUNK = 64
RWKV_N = 64
RWKV_CHUNK = 64
RWKV_LNX_EPS = 64e-5
RMS_EPS = 1e-6
L2_EPS = 1e-6

LANES = 128
SUBLANES = 8
VMEM_LIMIT = 56 * 1024 * 1024


def _cparams(sem):
    return pltpu.CompilerParams(dimension_semantics=sem, vmem_limit_bytes=VMEM_LIMIT)


def _pick(n, prefs):
    for p in prefs:
        if n % p == 0:
            return p
    return n


def _dot(a, b, precision=None):
    return jnp.dot(a, b, preferred_element_type=F32, precision=precision)


def _sigmoid(x):
    return 1.0 / (1.0 + jnp.exp(-x))


def _silu(x):
    half = 0.5 * x
    return half + half * jnp.tanh(half)


def _softplus(x):
    return jnp.maximum(x, 0.0) + jnp.log1p(jnp.exp(-jnp.abs(x)))


def _tri_masks(c):
    ii = lax.broadcasted_iota(jnp.int32, (c, c), 0)
    jj = lax.broadcasted_iota(jnp.int32, (c, c), 1)
    return ii >= jj, ii > jj, ii == jj


def _bdot(a, b):
    return lax.dot_general(a, b, (((2,), (1,)), ((0,), (0,))), preferred_element_type=F32)


def _bdot_nt(a, b):
    return lax.dot_general(a, b, (((2,), (2,)), ((0,), (0,))), preferred_element_type=F32)


def _bdot_tn(a, b):
    return lax.dot_general(a, b, (((1,), (1,)), ((0,), (0,))), preferred_element_type=F32)


def _mxu(x, rows):
    return x.astype(BF16) if rows >= 2 * SUBLANES else x


def _chunk_cumsum(x, c):
    incl, _, _ = _tri_masks(c)
    tri = jnp.broadcast_to(jnp.where(incl, 1.0, 0.0).astype(BF16)[None], (x.shape[0], c, c))
    hi = x.astype(BF16)
    rest = x - hi.astype(F32)
    mid = rest.astype(BF16)
    lo = (rest - mid.astype(F32)).astype(BF16)
    return _bdot(tri, hi) + _bdot(tri, mid) + _bdot(tri, lo)


def _pair_masks(c):
    left = lax.broadcasted_iota(jnp.int32, (1, 1, 2 * c), 2) < c
    ii = lax.broadcasted_iota(jnp.int32, (c, 2 * c), 0)
    jj = lax.broadcasted_iota(jnp.int32, (c, 2 * c), 1)
    jm = jnp.where(jj < c, jj, jj - c)
    return left, ii >= jm, ii > jm, ii == jm


def _bd(x, left):
    return jnp.concatenate([jnp.where(left, x, 0.0), jnp.where(left, 0.0, x)], axis=1)


def _pair_inverse(a, c):
    left, _, _, diag = _pair_masks(c)
    x = jnp.where(diag, 1.0, 0.0).astype(F32)[None] + a
    pw = _bdot(_mxu(a, c), _mxu(_bd(a, left), c))
    power = 2
    while power < c:
        if 2 * power < c:
            y = _bdot(_mxu(pw, c), _mxu(jnp.concatenate([_bd(x, left), _bd(pw, left)], axis=2), c))
            x = x + y[:, :, :2 * c]
            pw = y[:, :, 2 * c:]
        else:
            x = x + _bdot(_mxu(pw, c), _mxu(_bd(x, left), c))
        power *= 2
    return x


def _take(x, idx):
    return jnp.stack([x[i] for i in idx])


def _mm_kernel(*refs, nk, has_res):
    if has_res:
        a_ref, b_ref, r_ref, o_ref = refs[:4]
        scratch = refs[4:]
    else:
        a_ref, b_ref, o_ref = refs[:3]
        r_ref = None
        scratch = refs[3:]
    part = _dot(a_ref[...], b_ref[...])

    def finish(total):
        if has_res:
            total = total + r_ref[...]
        o_ref[...] = total.astype(o_ref.dtype)

    if nk == 1:
        finish(part)
        return
    acc_ref = scratch[0]
    k = pl.program_id(2)

    @pl.when(k == 0)
    def _():
        acc_ref[...] = part

    @pl.when(jnp.logical_and(k > 0, k < nk - 1))
    def _():
        acc_ref[...] += part

    @pl.when(k == nk - 1)
    def _():
        finish(acc_ref[...] + part)


def matmul(a, b, layer, res=None, out_dtype=F32, bm=None, bn=None, bk=None):
    m, kdim = a.shape
    n = b.shape[2]
    bm = bm or _pick(m, (1024, 512, 256, 128, 64, 32, 16, 8))
    bk = bk or kdim
    bn = bn or _pick(n, (1024, 512, 256, 128) if (res is None and bk == kdim) else (512, 256, 128))
    nk = kdim // bk
    in_specs = [pl.BlockSpec((bm, bk), lambda i, j, k: (i, k)),
                pl.BlockSpec((None, bk, bn), lambda i, j, k: (layer, k, j))]
    args = [a, b]
    if res is not None:
        in_specs.append(pl.BlockSpec((bm, bn), lambda i, j, k: (i, j)))
        args.append(res)
    scratch = [pltpu.VMEM((bm, bn), F32)] if nk > 1 else []
    return pl.pallas_call(
        functools.partial(_mm_kernel, nk=nk, has_res=res is not None),
        grid=(m // bm, n // bn, nk),
        in_specs=in_specs,
        out_specs=pl.BlockSpec((bm, bn), lambda i, j, k: (i, j)),
        out_shape=jax.ShapeDtypeStruct((m, n), out_dtype),
        scratch_shapes=scratch,
        compiler_params=_cparams(("parallel", "parallel", "arbitrary")),
        name="matmul",
    )(*args)


def _mm_ws_kernel(*refs, has_res):
    if has_res:
        a_ref, w_ref, r_ref, o_ref, wb_ref = refs
    else:
        a_ref, w_ref, o_ref, wb_ref = refs
        r_ref = None

    @pl.when(pl.program_id(1) == 0)
    def _():
        wb_ref[...] = w_ref[...].astype(BF16)

    total = _dot(a_ref[...], wb_ref[...])
    if has_res:
        total = total + r_ref[...]
    o_ref[...] = total.astype(o_ref.dtype)


def matmul_ws(a, w, layer, n_out=None, col0=0, res=None, out_dtype=F32, bn=None):
    m, kdim = a.shape
    n_out = n_out or w.shape[2]
    bm = _pick(m, (1024, 512, 256, 128, 64, 32, 16, 8))
    bn = bn or _pick(n_out, (512, 256, 128))
    assert col0 % bn == 0 and n_out % bn == 0
    off = col0 // bn
    in_specs = [pl.BlockSpec((bm, kdim), lambda j, i: (i, 0)),
                pl.BlockSpec((None, kdim, bn), lambda j, i: (layer, 0, off + j))]
    args = [a, w]
    if res is not None:
        in_specs.append(pl.BlockSpec((bm, bn), lambda j, i: (i, j)))
        args.append(res)
    return pl.pallas_call(
        functools.partial(_mm_ws_kernel, has_res=res is not None),
        grid=(n_out // bn, m // bm),
        in_specs=in_specs,
        out_specs=pl.BlockSpec((bm, bn), lambda j, i: (i, j)),
        out_shape=jax.ShapeDtypeStruct((m, n_out), out_dtype),
        scratch_shapes=[pltpu.VMEM((kdim, bn), BF16)],
        compiler_params=_cparams(("arbitrary", "arbitrary")),
        name="matmul_ws",
    )(*args)


def _gate_up_kernel(h_ref, wg_ref, wu_ref, o_ref, wgb_ref, wub_ref):
    @pl.when(pl.program_id(1) == 0)
    def _():
        wgb_ref[...] = wg_ref[...].astype(BF16)
        wub_ref[...] = wu_ref[...].astype(BF16)

    h = h_ref[...]
    g = _dot(h, wgb_ref[...])
    u = _dot(h, wub_ref[...])
    o_ref[...] = (_silu(g) * u).astype(o_ref.dtype)


def gate_up(h, wg, wu, layer):
    m, kdim = h.shape
    n = wg.shape[2]
    bm = _pick(m, (1024, 512, 256, 128, 64, 32, 16, 8))
    bn = _pick(n, (256, 128))
    wspec = pl.BlockSpec((None, kdim, bn), lambda j, i: (layer, 0, j))
    return pl.pallas_call(
        _gate_up_kernel,
        grid=(n // bn, m // bm),
        in_specs=[pl.BlockSpec((bm, kdim), lambda j, i: (i, 0)), wspec, wspec],
        out_specs=pl.BlockSpec((bm, bn), lambda j, i: (i, j)),
        out_shape=jax.ShapeDtypeStruct((m, n), BF16),
        scratch_shapes=[pltpu.VMEM((kdim, bn), BF16), pltpu.VMEM((kdim, bn), BF16)],
        compiler_params=_cparams(("arbitrary", "arbitrary")),
        name="ffn_gate_up",
    )(h, wg, wu)


def _rms_kernel(x_ref, w_ref, o_ref):
    x = x_ref[...]
    y = x * lax.rsqrt(jnp.mean(x * x, axis=-1, keepdims=True) + RMS_EPS)
    o_ref[...] = (y * w_ref[...]).astype(o_ref.dtype)


def rmsnorm(x, w, out_dtype):
    m, d = x.shape
    br = _pick(m, (512, 256, 128, 64, 32, 16, 8))
    return pl.pallas_call(
        _rms_kernel,
        grid=(m // br,),
        in_specs=[pl.BlockSpec((br, d), lambda i: (i, 0)),
                  pl.BlockSpec((1, d), lambda i: (0, 0))],
        out_specs=pl.BlockSpec((br, d), lambda i: (i, 0)),
        out_shape=jax.ShapeDtypeStruct((m, d), out_dtype),
        compiler_params=_cparams(("parallel",)),
        name="rmsnorm",
    )(x, w.reshape(1, d))


def _gdn_prep_kernel(x_ref, buf_ref, cw_ref, o_ref, win_ref, *, lb, n_q_blocks, n_qk_blocks, q_scale):
    cblk = pl.program_id(1)
    l = pl.program_id(2)
    halo = GDN_CONV_W - 1
    top = SUBLANES

    @pl.when(l == 0)
    def _():
        win_ref[:, top - halo:top, :] = buf_ref[...]

    win_ref[:, top:top + lb, :] = x_ref[...]
    acc = win_ref[:, top - halo:top - halo + lb, :] * cw_ref[0:1, :]
    for j in range(1, GDN_CONV_W):
        acc = acc + win_ref[:, top - halo + j:top - halo + j + lb, :] * cw_ref[j:j + 1, :]
    carry = win_ref[:, top + lb - halo:top + lb, :]
    win_ref[:, top - halo:top, :] = carry
    y = _silu(acc)

    @pl.when(cblk >= n_qk_blocks)
    def _():
        o_ref[...] = y

    @pl.when(cblk < n_qk_blocks)
    def _():
        scale = jnp.where(cblk < n_q_blocks, q_scale, 1.0).astype(F32)
        for s in range(y.shape[-1] // GDN_DK):
            ys = y[:, :, s * GDN_DK:(s + 1) * GDN_DK]
            inv = lax.rsqrt(jnp.sum(ys * ys, axis=-1, keepdims=True) + L2_EPS) * scale
            o_ref[:, :, s * GDN_DK:(s + 1) * GDN_DK] = ys * inv


def gdn_prep(proj, conv_buf, conv_w):
    b, l, _ = proj.shape
    kdim = GDN_HK * GDN_DK
    conv_dim = conv_w.shape[1]
    cb = _pick(kdim, (2048, 1024, 512, 256, 128))
    lb = _pick(l, (256, 128, 64, 32, 16, 8))
    bb = _pick(b, (32, 16, 8, 4, 2, 1)) if l < SUBLANES else 1
    kern = functools.partial(_gdn_prep_kernel, lb=lb, n_q_blocks=kdim // cb, n_qk_blocks=2 * kdim // cb,
                             q_scale=float(GDN_DK) ** -0.5)
    return pl.pallas_call(
        kern,
        grid=(b // bb, conv_dim // cb, l // lb),
        in_specs=[pl.BlockSpec((bb, lb, cb), lambda i, c, t: (i, t, c)),
                  pl.BlockSpec((bb, GDN_CONV_W - 1, cb), lambda i, c, t: (i, 0, c)),
                  pl.BlockSpec((GDN_CONV_W, cb), lambda i, c, t: (0, c))],
        out_specs=pl.BlockSpec((bb, lb, cb), lambda i, c, t: (i, t, c)),
        out_shape=jax.ShapeDtypeStruct((b, l, conv_dim), F32),
        scratch_shapes=[pltpu.VMEM((bb, SUBLANES + lb, cb), F32)],
        compiler_params=_cparams(("parallel", "parallel", "arbitrary")),
        name="gdn_prep",
    )(proj, conv_buf, conv_w)


def _in_proj_conv_kernel(a_ref, w_ref, buf_ref, cw_ref, o_ref, st_ref, wb_ref, win_ref, *,
                         bm, bn, blocks_per_seq, n_q_blocks, n_qk_blocks, q_scale):
    j = pl.program_id(0)
    i = pl.program_id(1)
    halo = GDN_CONV_W - 1
    top = SUBLANES

    @pl.when(i == 0)
    def _():
        wb_ref[...] = w_ref[...].astype(BF16)

    @pl.when(i % blocks_per_seq == 0)
    def _():
        win_ref[top - halo:top, :] = buf_ref[0]

    win_ref[top:top + bm, :] = _dot(a_ref[...], wb_ref[...])
    acc = win_ref[top - halo:top - halo + bm, :] * cw_ref[0:1, :]
    for tap in range(1, GDN_CONV_W):
        acc = acc + win_ref[top - halo + tap:top - halo + tap + bm, :] * cw_ref[tap:tap + 1, :]
    y = _silu(acc)
    scale = jnp.where(j < n_q_blocks, q_scale, 1.0).astype(F32)
    normed = j < n_qk_blocks
    for s0 in range(0, bn, GDN_DK):
        ys = y[:, s0:s0 + GDN_DK]
        inv = lax.rsqrt(jnp.sum(ys * ys, axis=-1, keepdims=True) + L2_EPS) * scale
        o_ref[:, s0:s0 + GDN_DK] = ys * jnp.where(normed, inv, 1.0)
    last = win_ref[top + bm - halo:top + bm, :]
    st_ref[0] = last
    win_ref[top - halo:top, :] = last


def gdn_in_proj_conv(h, w_in, conv_buf, conv_w, b, l):
    m, kdim = h.shape
    conv_dim = conv_w.shape[1]
    key_dim = GDN_HK * GDN_DK
    bm = _pick(l, (1024, 512, 256, 128, 64))
    bn = _pick(key_dim, (512, 256))
    bps = l // bm
    kern = functools.partial(_in_proj_conv_kernel, bm=bm, bn=bn, blocks_per_seq=bps, n_q_blocks=key_dim // bn,
                             n_qk_blocks=2 * key_dim // bn, q_scale=float(GDN_DK) ** -0.5)
    state_spec = pl.BlockSpec((1, GDN_CONV_W - 1, bn), lambda j, i: (i // bps, 0, j))
    return pl.pallas_call(
        kern,
        grid=(conv_dim // bn, m // bm),
        in_specs=[pl.BlockSpec((bm, kdim), lambda j, i: (i, 0)),
                  pl.BlockSpec((None, kdim, bn), lambda j, i: (0, 0, j)),
                  state_spec,
                  pl.BlockSpec((GDN_CONV_W, bn), lambda j, i: (0, j))],
        out_specs=[pl.BlockSpec((bm, bn), lambda j, i: (i, j)), state_spec],
        out_shape=[jax.ShapeDtypeStruct((m, conv_dim), F32),
                   jax.ShapeDtypeStruct((b, GDN_CONV_W - 1, conv_dim), F32)],
        scratch_shapes=[pltpu.VMEM((kdim, bn), BF16), pltpu.VMEM((SUBLANES + bm, bn), F32)],
        compiler_params=_cparams(("arbitrary", "arbitrary")),
        name="gdn_in_proj_conv",
    )(h, w_in, conv_buf, conv_w)


def _gdn_gates_kernel(ba_ref, alog_ref, dtb_ref, beta_ref, gc_ref, pad_ref, *, lb, lpb, chunk):
    hv = GDN_HV
    ba = ba_ref[0]
    beta = _sigmoid(ba[:, :hv])
    g = -jnp.exp(alog_ref[...]) * _softplus(ba[:, hv:2 * hv] + dtb_ref[...])
    if lpb != lb:
        pad_ref[...] = jnp.zeros_like(pad_ref)
        pad_ref[0:lb, :] = g
        g = pad_ref[...]
        pad_ref[0:lb, :] = beta
        beta = pad_ref[...]
    ii = lax.broadcasted_iota(jnp.int32, (lpb, lpb), 0)
    jj = lax.broadcasted_iota(jnp.int32, (lpb, lpb), 1)
    same_chunk_causal = jnp.logical_and(ii >= jj, ii // chunk == jj // chunk)
    gc = _dot(jnp.where(same_chunk_causal, 1.0, 0.0).astype(F32), g, HIGHEST)
    beta_ref[0] = beta
    gc_ref[0] = gc


def gdn_gates(ba, a_log, dt_bias, chunk, lp):
    b, l, w = ba.shape
    lb = _pick(l, (256, 128, 64, 32, 16, 8))
    lpb = lb if lp == l else lp
    kern = functools.partial(_gdn_gates_kernel, lb=lb, lpb=lpb, chunk=chunk)
    out = jax.ShapeDtypeStruct((b, lp, GDN_HV), F32)
    return pl.pallas_call(
        kern,
        grid=(b, l // lb),
        in_specs=[pl.BlockSpec((1, lb, w), lambda i, t: (i, t, 0)),
                  pl.BlockSpec((1, GDN_HV), lambda i, t: (0, 0)),
                  pl.BlockSpec((1, GDN_HV), lambda i, t: (0, 0))],
        out_specs=[pl.BlockSpec((1, lpb, GDN_HV), lambda i, t: (i, t, 0)),
                   pl.BlockSpec((1, lpb, GDN_HV), lambda i, t: (i, t, 0))],
        out_shape=[out, out],
        scratch_shapes=[pltpu.VMEM((lpb, GDN_HV), F32)],
        compiler_params=_cparams(("parallel", "parallel")),
        name="gdn_gates",
    )(ba, a_log.reshape(1, GDN_HV), dt_bias.reshape(1, GDN_HV))


def _gdn_delta_kernel(*refs, l_real, lb, chunk, hb, has_s0):
    if has_s0:
        q_ref, k_ref, v_ref, z_ref, beta_ref, gc_ref, gct_ref, nw_ref, s0_ref = refs[:9]
        rest = refs[9:]
    else:
        q_ref, k_ref, v_ref, z_ref, beta_ref, gc_ref, gct_ref, nw_ref = refs[:8]
        s0_ref = None
        rest = refs[8:]
    o_ref, so_ref, s_scr, padk_scr, padv_scr = rest
    hblk = pl.program_id(1)
    t = pl.program_id(2)
    nt = pl.num_programs(2)
    c = chunk
    nc = lb // c
    padded = l_real < lb
    rep = GDN_HV // GDN_HK
    kh = hb // rep
    pairs = hb // 2
    left_c, incl, strict, _ = _pair_masks(c)

    @pl.when(t == 0)
    def _():
        if has_s0:
            s_scr[...] = s0_ref[0]
        else:
            s_scr[...] = jnp.zeros_like(s_scr)

    def load(ref, pad_scr):
        if padded:
            pad_scr[...] = jnp.zeros_like(pad_scr)
            pad_scr[0:l_real, :] = ref[0]
            return pad_scr[...]
        return ref[0]

    q2 = load(q_ref, padk_scr)
    k2 = load(k_ref, padk_scr)
    v2 = load(v_ref, padv_scr)
    beta_t = beta_ref[0]
    gc_t = gc_ref[0]
    lane = lax.broadcasted_iota(jnp.int32, (lb, GDN_HV), 1)

    q3, k3, qk2, kk2 = [], [], [], []
    for p in range(kh):
        qp = q2[:, p * GDN_DK:(p + 1) * GDN_DK].reshape(nc, c, GDN_DK)
        kp = k2[:, p * GDN_DK:(p + 1) * GDN_DK].reshape(nc, c, GDN_DK)
        q3.append(qp)
        k3.append(kp)
        kp_twice = _mxu(jnp.concatenate([kp, kp], axis=1), c)
        qk2.append(_bdot_nt(_mxu(qp, c), kp_twice))
        kk2.append(_bdot_nt(_mxu(kp, c), kp_twice))

    def column(tile, head):
        return jnp.sum(jnp.where(lane == head, tile, 0.0), axis=-1, keepdims=True).reshape(nc, c, 1)

    zero = jnp.zeros((nc, c, GDN_DV + GDN_DK), F32)
    m_l, rhs_l, ai_l = [], [], []
    wq_l, kg_l, gl_l = [], [], []
    for pp in range(pairs):
        kp, qp = k3[2 * pp // rep], q3[2 * pp // rep]
        beta = [column(beta_t, hblk * hb + 2 * pp + e) for e in range(2)]
        gcol = [column(gc_t, hblk * hb + 2 * pp + e) for e in range(2)]
        grow = gct_ref[0, pp]
        diff = jnp.where(left_c, gcol[0], gcol[1]) - grow
        decay = jnp.where(incl, jnp.exp(jnp.where(incl, diff, 0.0)), 0.0)
        ai_l.append(_mxu(qk2[2 * pp // rep] * decay, c))
        m_l.append(jnp.where(strict, kk2[2 * pp // rep] * (jnp.where(left_c, beta[0], beta[1]) * decay), 0.0))
        blocks = []
        for e in range(2):
            j = 2 * pp + e
            eg = jnp.exp(gcol[e])
            g_last = gcol[e][:, c - 1:c, :]
            vj = v2[:, j * GDN_DV:(j + 1) * GDN_DV].reshape(nc, c, GDN_DV)
            blocks.append(jnp.concatenate([vj * beta[e], kp * (beta[e] * eg)], axis=-1))
            wq_l.append(qp * eg)
            kg_l.append(_mxu(kp * jnp.exp(g_last - gcol[e]), c))
            gl_l.append(jnp.exp(g_last))
        rhs_l.append(jnp.concatenate([jnp.concatenate([blocks[0], zero], axis=-1),
                                      jnp.concatenate([zero, blocks[1]], axis=-1)], axis=1))
    cat = lambda xs: jnp.concatenate(xs, axis=0)
    tinv = _mxu(_pair_inverse(-cat(m_l), c), c)
    uw = _bdot(tinv, _mxu(cat(rhs_l), c))
    a_intra = cat(ai_l)
    width = GDN_DV + GDN_DK
    u_l, wq_rows = [], []
    for pp in range(pairs):
        for e in range(2):
            part = uw[pp * nc:(pp + 1) * nc, :, e * width:(e + 1) * width]
            u_l.append(part[:, :, :GDN_DV])
            wq_rows.append(_mxu(jnp.concatenate([part[:, :, GDN_DV:], wq_l[2 * pp + e]], axis=1), c))

    s = s_scr[...]
    zero_v = jnp.zeros((pairs, c, GDN_DV), F32)
    for ci in range(nc):
        pick = lambda xs: jnp.stack([x[ci] for x in xs])
        ws_qs = _bdot(pick(wq_rows), _mxu(s, c))
        v_new = pick(u_l) - ws_qs[:, :c]
        v_newb = _mxu(v_new, c)
        v_even = jnp.stack([v_new[2 * pp] for pp in range(pairs)])
        v_odd = jnp.stack([v_new[2 * pp + 1] for pp in range(pairs)])
        v_bd = jnp.concatenate([jnp.concatenate([v_even, zero_v], axis=-1),
                                jnp.concatenate([zero_v, v_odd], axis=-1)], axis=1)
        av = _bdot(_take(a_intra, [pp * nc + ci for pp in range(pairs)]), _mxu(v_bd, c))
        o = ws_qs[:, c:] + jnp.stack([av[j // 2][:, (j % 2) * GDN_DV:(j % 2 + 1) * GDN_DV] for j in range(hb)])
        s = s * pick(gl_l) + _bdot_tn(pick(kg_l), v_newb)
        o = o * lax.rsqrt(jnp.mean(o * o, axis=-1, keepdims=True) + RMS_EPS) * nw_ref[...]
        for j in range(hb):
            vslab = slice(j * GDN_DV, (j + 1) * GDN_DV)
            if padded:
                z = z_ref[0, :, vslab]
                o_ref[0, :, vslab] = (o[j][0:l_real] * _silu(z)).astype(o_ref.dtype)
            else:
                z = z_ref[0, ci * c:(ci + 1) * c, vslab]
                o_ref[0, ci * c:(ci + 1) * c, vslab] = (o[j] * _silu(z)).astype(o_ref.dtype)
    s_scr[...] = s

    @pl.when(t == nt - 1)
    def _():
        so_ref[0] = s


def gdn_delta(qkv, zsrc, z_col0, beta, gc, gct, norm_w, s0, l_real, chunk, lb, hb):
    b = qkv.shape[0]
    lp = beta.shape[1]
    lrows = l_real if l_real < lb else lb
    nt = lp // lb
    rep = GDN_HV // GDN_HK
    kh = hb // rep
    k_base = GDN_HK // kh
    v_base = (2 * GDN_HK * GDN_DK) // (hb * GDN_DV)
    z_base = z_col0 // (hb * GDN_DV)
    in_specs = [
        pl.BlockSpec((1, lrows, kh * GDN_DK), lambda i, h, t: (i, t, h)),
        pl.BlockSpec((1, lrows, kh * GDN_DK), lambda i, h, t: (i, t, k_base + h)),
        pl.BlockSpec((1, lrows, hb * GDN_DV), lambda i, h, t: (i, t, v_base + h)),
        pl.BlockSpec((1, lrows, hb * GDN_DV), lambda i, h, t: (i, t, z_base + h)),
        pl.BlockSpec((1, lb, GDN_HV), lambda i, h, t: (i, t, 0)),
        pl.BlockSpec((1, lb, GDN_HV), lambda i, h, t: (i, t, 0)),
        pl.BlockSpec((1, hb // 2, lb // chunk, 1, 2 * chunk), lambda i, h, t: (i, h, t, 0, 0)),
        pl.BlockSpec((1, GDN_DV), lambda i, h, t: (0, 0)),
    ]
    args = [qkv, qkv, qkv, zsrc, beta, gc, gct, norm_w.reshape(1, GDN_DV)]
    if s0 is not None:
        in_specs.append(pl.BlockSpec((1, hb, GDN_DK, GDN_DV), lambda i, h, t: (i, h, 0, 0)))
        args.append(s0)
    kern = functools.partial(_gdn_delta_kernel, l_real=l_real, lb=lb, chunk=chunk, hb=hb, has_s0=s0 is not None)
    return pl.pallas_call(
        kern,
        grid=(b, GDN_HV // hb, nt),
        in_specs=in_specs,
        out_specs=[pl.BlockSpec((1, lrows, hb * GDN_DV), lambda i, h, t: (i, t, h)),
                   pl.BlockSpec((1, hb, GDN_DK, GDN_DV), lambda i, h, t: (i, h, 0, 0))],
        out_shape=[jax.ShapeDtypeStruct((b, l_real, GDN_HV * GDN_DV), BF16),
                   jax.ShapeDtypeStruct((b, GDN_HV, GDN_DK, GDN_DV), F32)],
        scratch_shapes=[pltpu.VMEM((hb, GDN_DK, GDN_DV), F32),
                        pltpu.VMEM((lb, kh * GDN_DK), F32),
                        pltpu.VMEM((lb, hb * GDN_DV), F32)],
        compiler_params=_cparams(("parallel", "parallel", "arbitrary")),
        name="gdn_delta",
    )(*args)


_LORA_ACTS = ("tanh", "none", "sigmoid")


def _lora_act(hmid, act):
    if act == "tanh":
        return jnp.tanh(hmid)
    if act == "sigmoid":
        return _sigmoid(hmid)
    return hmid


def _rwkv_mix_kernel(*refs, lb, fuse_lora):
    x_ref, nw_ref, shift_ref, mix_ref = refs[:4]
    if fuse_lora:
        f_refs = refs[4:7]
        outs = refs[7:]
        (o_r, o_k, o_v), mid_refs, last_ref, win_ref = outs[:3], outs[3:6], outs[6], outs[7]
    else:
        (o_r, o_w, o_k, o_v, o_a, o_g), last_ref, win_ref = refs[4:10], refs[10], refs[11]
    t = pl.program_id(1)
    top = SUBLANES

    @pl.when(t == 0)
    def _():
        win_ref[top - 1:top, :] = shift_ref[0]

    x = x_ref[0]
    h = x * lax.rsqrt(jnp.mean(x * x, axis=-1, keepdims=True) + RMS_EPS) * nw_ref[...]
    win_ref[top:top + lb, :] = h
    prev = win_ref[top - 1:top - 1 + lb, :]
    last = win_ref[top + lb - 1:top + lb, :]
    win_ref[top - 1:top, :] = last
    last_ref[0] = last
    xx = prev - h
    mixed = lambda i: (h + xx * mix_ref[i:i + 1, :]).astype(BF16)
    o_r[0], o_k[0], o_v[0] = mixed(0), mixed(2), mixed(3)
    if fuse_lora:
        for row, f_ref, m_ref, act in zip((1, 4, 5), f_refs, mid_refs, _LORA_ACTS):
            m_ref[0] = _lora_act(_dot(mixed(row), f_ref[...]), act)
    else:
        o_w[0], o_a[0], o_g[0] = mixed(1), mixed(4), mixed(5)


def rwkv_mix(x, norm_w, shift, mix, firsts):
    b, l, d = x.shape
    lb = _pick(l, (256, 128, 64, 32, 16, 8))
    fuse_lora = lb % SUBLANES == 0
    tok = pl.BlockSpec((1, lb, d), lambda i, t: (i, t, 0))
    row = pl.BlockSpec((1, 1, d), lambda i, t: (i, 0, 0))
    in_specs = [tok, pl.BlockSpec((1, d), lambda i, t: (0, 0)), row, pl.BlockSpec((6, d), lambda i, t: (0, 0))]
    args = [x, norm_w.reshape(1, d), shift.reshape(b, 1, d), mix]
    big = jax.ShapeDtypeStruct((b, l, d), BF16)
    if fuse_lora:
        in_specs += [pl.BlockSpec(f.shape, lambda i, t: (0, 0)) for f in firsts]
        args += list(firsts)
        out_specs = [tok] * 3 + [pl.BlockSpec((1, lb, f.shape[1]), lambda i, t: (i, t, 0)) for f in firsts]
        out_shape = [big] * 3 + [jax.ShapeDtypeStruct((b, l, f.shape[1]), F32) for f in firsts]
    else:
        out_specs = [tok] * 6
        out_shape = [big] * 6
    outs = pl.pallas_call(
        functools.partial(_rwkv_mix_kernel, lb=lb, fuse_lora=fuse_lora),
        grid=(b, l // lb),
        in_specs=in_specs,
        out_specs=out_specs + [row],
        out_shape=out_shape + [jax.ShapeDtypeStruct((b, 1, d), F32)],
        scratch_shapes=[pltpu.VMEM((SUBLANES + lb, d), F32)],
        compiler_params=_cparams(("parallel", "arbitrary")),
        name="rwkv_mix",
    )(*args)
    new_shift = outs[-1].reshape(b, d)
    if fuse_lora:
        return outs[:3], list(outs[3:6]), new_shift
    xr, xw, xk, xv, xa, xg = outs[:6]
    mids = [lora_mid(t.reshape(b * l, d), f, act).reshape(b, l, -1)
            for t, f, act in zip((xw, xa, xg), firsts, _LORA_ACTS)]
    return (xr, xk, xv), mids, new_shift


def _lora_kernel(x_ref, w1_ref, o_ref, *, act):
    o_ref[...] = _lora_act(_dot(x_ref[...], w1_ref[...]), act)


def lora_mid(x, w1, act):
    m, d = x.shape
    r = w1.shape[1]
    bm = _pick(m, (512, 256, 128, 64, 32, 16, 8))
    return pl.pallas_call(
        functools.partial(_lora_kernel, act=act),
        grid=(m // bm,),
        in_specs=[pl.BlockSpec((bm, d), lambda i: (i, 0)),
                  pl.BlockSpec((d, r), lambda i: (0, 0))],
        out_specs=pl.BlockSpec((bm, r), lambda i: (i, 0)),
        out_shape=jax.ShapeDtypeStruct((m, r), F32),
        compiler_params=_cparams(("parallel",)),
        name="rwkv_lora",
    )(x, w1)


def _rwkv_pair_kernel(*refs, l_real, lb, chunk, has_s0):
    n_in = 17 if has_s0 else 16
    (r_ref, k_ref, v_ref, hw_ref, ha_ref, hg_ref, w2_ref, a2_ref, g2_ref, w0_ref, a0_ref,
     kk_ref, ka_ref, rk_ref, lnw_ref, lnb_ref) = refs[:16]
    s0_ref = refs[16] if has_s0 else None
    o_ref, so_ref, s_scr, pad_scr, padm_scr = refs[n_in:]
    t = pl.program_id(2)
    nt = pl.num_programs(2)
    n = RWKV_N
    c = chunk
    nc = lb // c
    width = r_ref.shape[-1]
    pairs = width // (2 * n)
    padded = l_real < lb
    left_n = lax.broadcasted_iota(jnp.int32, (1, 1, 2 * n), 2) < n
    _, incl, strict, _ = _pair_masks(c)
    rr = lax.broadcasted_iota(jnp.int32, (2 * n, 2 * n), 0) < n
    cc = lax.broadcasted_iota(jnp.int32, (2 * n, 2 * n), 1) < n
    same_head = rr == cc
    bd = _bd

    def head_sum(x):
        sl = jnp.sum(jnp.where(left_n, x, 0.0), axis=-1, keepdims=True)
        sr = jnp.sum(jnp.where(left_n, 0.0, x), axis=-1, keepdims=True)
        return jnp.where(left_n, sl, sr)

    @pl.when(t == 0)
    def _():
        for p in range(pairs):
            if has_s0:
                zero = jnp.zeros((n, n), F32)
                top = jnp.concatenate([s0_ref[0, 2 * p], zero], axis=1)
                bot = jnp.concatenate([zero, s0_ref[0, 2 * p + 1]], axis=1)
                s_scr[p] = jnp.concatenate([top, bot], axis=0)
            else:
                s_scr[p] = jnp.zeros((2 * n, 2 * n), F32)

    def load(ref):
        if padded:
            pad_scr[...] = jnp.zeros_like(pad_scr)
            pad_scr[0:l_real, :] = ref[0]
            return pad_scr[...]
        return ref[0]

    def second_factor(h_ref, f_ref):
        rank = h_ref.shape[-1]
        if padded:
            padm_scr[...] = jnp.zeros_like(padm_scr)
            padm_scr[0:l_real, 0:rank] = h_ref[0]
            hmid = padm_scr[:, 0:rank]
        else:
            hmid = h_ref[0]
        return _dot(hmid.astype(BF16), f_ref[...])

    r2, k2, v2 = (load(ref) for ref in (r_ref, k_ref, v_ref))
    w2 = w0_ref[...] + second_factor(hw_ref, w2_ref)
    a2 = a0_ref[...] + second_factor(ha_ref, a2_ref)
    g2 = second_factor(hg_ref, g2_ref)
    lw2 = -jnp.exp(-_softplus(-w2) - 0.5)
    if padded:
        rows = lax.broadcasted_iota(jnp.int32, lw2.shape, 0)
        lw2 = jnp.where(rows < l_real, lw2, 0.0)
    a_sig = _sigmoid(a2)
    kmod = k2 * (1.0 + (a_sig - 1.0) * ka_ref[...])
    cl2 = _chunk_cumsum(lw2.reshape(nc, c, width), c).reshape(lb, width)

    def split(x2):
        x3 = x2.reshape(nc, c, width)
        return jnp.concatenate([x3[:, :, p * 2 * n:(p + 1) * 2 * n] for p in range(pairs)], axis=0)

    def per_pair(p_ref):
        return jnp.concatenate([jnp.broadcast_to(p_ref[:, p * 2 * n:(p + 1) * 2 * n][None], (nc, 1, 2 * n))
                                for p in range(pairs)], axis=0)

    kk = split(k2 * kk_ref[...])
    kk = kk * lax.rsqrt(head_sum(kk * kk) + L2_EPS)
    ep = split(jnp.exp(cl2))
    em = split(jnp.exp(-cl2))
    a_h = -kk * split(jnp.exp(cl2 - lw2))
    b_h = kk * split(a_sig) * em
    r = split(r2)
    k = split(kmod)
    v = split(v2)
    r_h = r * ep
    k_h = k * em
    pc = ep[:, c - 1:c, :]

    amat = _bdot_nt(_mxu(jnp.concatenate([a_h, r_h], axis=1), c),
                    _mxu(jnp.concatenate([bd(k_h, left_n), bd(b_h, left_n)], axis=1), c))
    a_ak = jnp.where(strict, amat[:, :c, :2 * c], 0.0)
    a_ab = jnp.where(strict, amat[:, :c, 2 * c:], 0.0)
    a_rk = jnp.where(incl, amat[:, c:, :2 * c], 0.0)
    a_rb = jnp.where(incl, amat[:, c:, 2 * c:], 0.0)

    tinv = _mxu(_pair_inverse(a_ab, c), c)

    vbd = bd(v, left_n)
    akv = _bdot(_mxu(a_ak, c), _mxu(vbd, c))
    wu = _bdot(tinv, _mxu(jnp.concatenate([bd(a_h, left_n), bd(akv, left_n)], axis=2), c))
    wmat = wu[:, :, :2 * n]
    ut = wu[:, :, 2 * n:]
    top = jnp.concatenate([bd(wmat, left_n), bd(ut, left_n)], axis=2)
    bot = jnp.concatenate([jnp.zeros_like(vbd), vbd], axis=2)
    ry = _bdot(_mxu(jnp.concatenate([a_rb, a_rk], axis=2), c), _mxu(jnp.concatenate([top, bot], axis=1), c))
    rt = r_h + ry[:, :, :2 * n]
    yt = ry[:, :, 2 * n:]
    bt = b_h * pc
    btb = _mxu(bt, c)
    hmat = _bdot_tn(_mxu(jnp.concatenate([ut, v], axis=1), c),
                    _mxu(jnp.concatenate([bt, k_h * pc], axis=1), c))
    hmat = jnp.where(same_head, hmat, 0.0)
    rw = _mxu(jnp.concatenate([rt, wmat], axis=1), c)

    s = s_scr[...]
    ys = [None] * (pairs * nc)
    for ci in range(nc):
        idx = [p * nc + ci for p in range(pairs)]
        ys_sw = _bdot_nt(_take(rw, idx), _mxu(s, c))
        y_c = ys_sw[:, :c] + _take(yt, idx)
        swt = _mxu(ys_sw[:, c:], c)
        s = s * _take(pc, idx) + jnp.where(same_head, _bdot_tn(swt, _take(btb, idx)), 0.0) + _take(hmat, idx)
        for j, i in enumerate(idx):
            ys[i] = y_c[j]
    s_scr[...] = s
    y = jnp.stack(ys)

    yc = y - head_sum(y) * (1.0 / n)
    var = head_sum(yc * yc) * (1.0 / n)
    yn = yc * lax.rsqrt(var + RWKV_LNX_EPS) * per_pair(lnw_ref) + per_pair(lnb_ref)
    yn = yn + head_sum(r * k * per_pair(rk_ref)) * v
    out = (yn * split(g2)).astype(o_ref.dtype)
    for p in range(pairs):
        rows = out[p * nc:(p + 1) * nc].reshape(lb, 2 * n)
        if padded:
            o_ref[0, :, p * 2 * n:(p + 1) * 2 * n] = rows[0:l_real]
        else:
            o_ref[0, :, p * 2 * n:(p + 1) * 2 * n] = rows

    @pl.when(t == nt - 1)
    def _():
        for p in range(pairs):
            so_ref[0, 2 * p] = s[p][:n, :n]
            so_ref[0, 2 * p + 1] = s[p][n:, n:]


def rwkv_wkv(r, k, v, mids, seconds, w0, a0, k_k, k_a, r_k, lnx_w, lnx_b, s0, chunk, lb, heads):
    b, l, d = r.shape
    n = RWKV_N
    h = d // n
    width = heads * n
    lrows = l if l < lb else lb
    nt = max(1, l // lb)
    tok = pl.BlockSpec((1, lrows, width), lambda i, p, t: (i, t, p))
    par = pl.BlockSpec((1, width), lambda i, p, t: (0, p))
    st = pl.BlockSpec((1, heads, n, n), lambda i, p, t: (i, p, 0, 0))
    mid_specs = [pl.BlockSpec((1, lrows, m.shape[-1]), lambda i, p, t: (i, t, 0)) for m in mids]
    sec_specs = [pl.BlockSpec((s.shape[0], width), lambda i, p, t: (0, p)) for s in seconds]
    in_specs = [tok] * 3 + mid_specs + sec_specs + [par] * 7
    args = [r, k, v, *mids, *seconds, w0.reshape(1, d), a0.reshape(1, d), k_k.reshape(1, d), k_a.reshape(1, d),
            r_k.reshape(1, d), lnx_w.reshape(1, d), lnx_b.reshape(1, d)]
    if s0 is not None:
        in_specs.append(st)
        args.append(s0)
    kern = functools.partial(_rwkv_pair_kernel, l_real=l, lb=lb, chunk=chunk, has_s0=s0 is not None)
    return pl.pallas_call(
        kern,
        grid=(b, h // heads, nt),
        in_specs=in_specs,
        out_specs=[tok, st],
        out_shape=[jax.ShapeDtypeStruct((b, l, d), BF16),
                   jax.ShapeDtypeStruct((b, h, n, n), F32)],
        scratch_shapes=[pltpu.VMEM((heads // 2, 2 * n, 2 * n), F32), pltpu.VMEM((lb, width), F32),
                        pltpu.VMEM((lb, max(m.shape[-1] for m in mids)), F32)],
        compiler_params=_cparams(("parallel", "parallel", "arbitrary")),
        name="rwkv_wkv",
    )(*args)


def _round_up(x, m):
    return -(-x // m) * m


def _gdn_layer(x, conv_st, ssm_st, w):
    b, l, d = x.shape
    conv_dim = 2 * GDN_HK * GDN_DK + GDN_HV * GDN_DV
    x2 = x.reshape(b * l, d)
    h = rmsnorm(x2, w["norm_mix"], BF16)
    vdim = GDN_HV * GDN_DV
    main_cols = conv_dim + vdim
    ba_cols = _round_up(2 * GDN_HV, LANES)
    ba = matmul_ws(h, w["w_in"], 0, n_out=ba_cols, col0=main_cols, bn=LANES).reshape(b, l, -1)
    if conv_st is None:
        conv_st = jnp.zeros((b, GDN_CONV_W - 1, conv_dim), F32)
    if l >= GDN_CHUNK:
        qkv, new_conv = gdn_in_proj_conv(h, w["w_in"], conv_st, w["conv_w"], b, l)
        qkv = qkv.reshape(b, l, conv_dim)
        zsrc, z_col0 = matmul_ws(h, w["w_in"], 0, n_out=vdim, col0=conv_dim).reshape(b, l, vdim), 0
    else:
        proj = matmul_ws(h, w["w_in"], 0, n_out=main_cols).reshape(b, l, -1)
        new_conv = proj[:, l - (GDN_CONV_W - 1):, :conv_dim]
        qkv = gdn_prep(proj, conv_st, w["conv_w"])
        zsrc, z_col0 = proj, conv_dim
    chunk = GDN_CHUNK if l >= GDN_CHUNK else _round_up(l, SUBLANES)
    lp = _round_up(l, chunk)
    beta, gc = gdn_gates(ba, w["a_log"], w["dt_bias"], chunk, lp)
    gct = jnp.swapaxes(gc, 1, 2).reshape(b, GDN_HV // 2, 2, lp // chunk, chunk)
    gct = jnp.transpose(gct, (0, 1, 3, 2, 4)).reshape(b, GDN_HV // 2, lp // chunk, 1, 2 * chunk)
    if l >= GDN_CHUNK:
        lb, hb = _pick(lp, (512, 256, 128, 64)), _pick(GDN_HV, (16, 8, 4, 2))
    else:
        lb, hb = lp, _pick(GDN_HV, (64, 32, 16, 8, 4, 2))
    o, s_new = gdn_delta(qkv, zsrc, z_col0, beta, gc, gct, w["norm_w"], ssm_st, l, chunk, lb, hb)
    x2 = matmul(o.reshape(b * l, -1), w["w_out"], 0, res=x2, bk=_pick(o.shape[-1], (4096,)))
    return x2.reshape(b, l, d), new_conv, s_new


def _rwkv_layer(x, shift_st, wkv_st, w):
    b, l, d = x.shape
    if shift_st is None:
        shift_st = jnp.zeros((b, d), F32)
    (xr, xk, xv), mids, new_shift = rwkv_mix(x, w["norm_mix"], shift_st, w["mix"], [w["w1"], w["a1"], w["g1"]])
    m = b * l
    flat = lambda t: t.reshape(m, d)
    r = matmul_ws(flat(xr), w["w_r"], 0)
    k = matmul_ws(flat(xk), w["w_k"], 0)
    v = matmul_ws(flat(xv), w["w_v"], 0)
    chunk = RWKV_CHUNK if l >= RWKV_CHUNK else _round_up(l, SUBLANES)
    n_heads = d // RWKV_N
    if l >= RWKV_CHUNK:
        lb, heads = _pick(l, (512, 256, 128, 64)), _pick(n_heads, (16, 8, 4, 2))
    else:
        lb, heads = chunk, _pick(n_heads, (64, 32, 16, 8, 4, 2))
    to3 = lambda t: t.reshape(b, l, d)
    y, s_new = rwkv_wkv(to3(r), to3(k), to3(v), mids, [w["w2"], w["a2"], w["g2"]], w["w0"], w["a0"],
                        w["k_k"], w["k_a"], w["r_k"], w["lnx_w"], w["lnx_b"], wkv_st, chunk, lb, heads)
    x2 = matmul_ws(y.reshape(m, d), w["w_out"], 0, res=x.reshape(m, d))
    return x2.reshape(b, l, d), new_shift, s_new


def _ffn(x, w, layer):
    b, l, d = x.shape
    x2 = x.reshape(b * l, d)
    h = rmsnorm(x2, w["norm"][layer], BF16)
    act = gate_up(h, w["w_gate"], w["w_up"], layer)
    dff = act.shape[1]
    bk = dff // 2 if (dff // 2) % LANES == 0 else dff
    x2 = matmul(act, w["w_down"], layer, res=x2, bn=_pick(d, (512, 256, 128)), bk=bk)
    return x2.reshape(b, l, d)


def _trunk(x, conv_st, ssm_st, shift_st, wkv_st, wts):
    x, conv_new, ssm_new = _gdn_layer(x, conv_st, ssm_st, wts["gdn"])
    x = _ffn(x, wts["ffn"], 0)
    x, shift_new, wkv_new = _rwkv_layer(x, shift_st, wkv_st, wts["rwkv"])
    x = _ffn(x, wts["ffn"], 1)
    b, l, d = x.shape
    y = rmsnorm(x.reshape(b * l, d), wts["norm_final"], F32).reshape(b, l, d)
    return y, conv_new[None], ssm_new[None], shift_new[None], wkv_new[None]


def _pad_to(x, axis, size):
    pad = size - x.shape[axis]
    if pad == 0:
        return x
    widths = [(0, 0)] * x.ndim
    widths[axis] = (0, pad)
    return jnp.pad(x, widths)


def kernel(x_prompt, x_sample, state_gdn_conv, state_gdn_ssm, state_rwkv_shift, state_rwkv_wkv,
           norm_mix, norm_ffn, norm_final,
           gdn_w_in, gdn_conv_w, gdn_a_log, gdn_dt_bias, gdn_norm_w, gdn_w_out,
           rwkv_mix, rwkv_w_r, rwkv_w_k, rwkv_w_v, rwkv_w0, rwkv_w1, rwkv_w2,
           rwkv_a0, rwkv_a1, rwkv_a2, rwkv_g1, rwkv_g2, rwkv_k_k, rwkv_k_a, rwkv_r_k,
           rwkv_lnx_w, rwkv_lnx_b, rwkv_w_out,
           ffn_w_gate, ffn_w_up, ffn_w_down):
    bf = lambda t: t.astype(BF16)
    gate_rank = _round_up(rwkv_g1.shape[-1], LANES)
    wts = {
        "gdn": {
            "norm_mix": norm_mix[0],
            "w_in": _pad_to(gdn_w_in, 2, _round_up(gdn_w_in.shape[2], LANES)),
            "conv_w": gdn_conv_w[0], "a_log": gdn_a_log[0], "dt_bias": gdn_dt_bias[0],
            "norm_w": gdn_norm_w[0], "w_out": bf(gdn_w_out),
        },
        "rwkv": {
            "norm_mix": norm_mix[1], "mix": rwkv_mix[0],
            "w_r": rwkv_w_r, "w_k": rwkv_w_k, "w_v": rwkv_w_v,
            "w0": rwkv_w0[0], "w1": bf(rwkv_w1[0]), "w2": bf(rwkv_w2[0]),
            "a0": rwkv_a0[0], "a1": bf(rwkv_a1[0]), "a2": bf(rwkv_a2[0]),
            "g1": bf(_pad_to(rwkv_g1[0], 1, gate_rank)), "g2": bf(_pad_to(rwkv_g2[0], 0, gate_rank)),
            "k_k": rwkv_k_k[0], "k_a": rwkv_k_a[0], "r_k": rwkv_r_k[0],
            "lnx_w": rwkv_lnx_w[0], "lnx_b": rwkv_lnx_b[0], "w_out": rwkv_w_out,
        },
        "ffn": {"norm": norm_ffn, "w_gate": ffn_w_gate, "w_up": ffn_w_up, "w_down": bf(ffn_w_down)},
        "norm_final": norm_final,
    }
    outs_p = _trunk(x_prompt, None, None, None, None, wts)
    outs_s = _trunk(x_sample, state_gdn_conv[0], state_gdn_ssm[0], state_rwkv_shift[0], state_rwkv_wkv[0], wts)
    return (outs_p[0], outs_s[0]) + tuple(outs_p[1:]) + tuple(outs_s[1:])
```
